```python
import math
import jax, jax.numpy as jnp
from jax import lax
import numpy as np

D_MODEL = 2048
BATCH = 16
SEQ = 2048
DEPTH = 2
DEC_BATCH = 2
DEC_SEQ = 8192
PAST_LEN = 128

HEAD_DIM_ATT = 128
N_Q_HEADS = D_MODEL // 256
N_KV_HEADS = N_Q_HEADS // 4
ATT_WIDTH = N_Q_HEADS * HEAD_DIM_ATT
KV_WIDTH = N_KV_HEADS * HEAD_DIM_ATT
WINDOW = 128
BLOCK = 128
N_BUCKETS = 32
MAX_DISTANCE = 128
N_RET_HEADS = D_MODEL // 512
RET_HEAD_DIM = 256
RET_WIDTH = N_RET_HEADS * RET_HEAD_DIM
CHUNK = 128
D_MIX = ATT_WIDTH + RET_WIDTH
IN_SPLITS = [ATT_WIDTH, KV_WIDTH, KV_WIDTH, RET_WIDTH, RET_WIDTH, RET_WIDTH, RET_WIDTH]
IN_WIDTH = sum(IN_SPLITS)
IN_OFFSETS = [int(o) for o in np.cumsum(IN_SPLITS)[:-1]]
N_EXPERTS = 16
CAPACITY_FACTOR = 2
EXPERT_FF = D_MODEL // 2
EPS = 1e-6
NEG_INF = -1e30

kernel_name = "hymba_window_gqa_retention_ec_moe_encoder"


def rms_norm(x, g):
    xf = x.astype(jnp.float32)
    var = jnp.mean(xf * xf, axis=-1, keepdims=True)
    return (xf * lax.rsqrt(var + EPS) * g.astype(jnp.float32)).astype(x.dtype)


def t5_bucket(rel):
    half = N_BUCKETS // 2
    max_exact = half // 2
    bucket = jnp.where(rel > 0, half, 0)
    n = jnp.abs(rel)
    nf = jnp.maximum(n, 1).astype(jnp.float32)
    large = max_exact + (jnp.log(nf / max_exact) / math.log(MAX_DISTANCE / max_exact)
                         * (half - max_exact)).astype(jnp.int32)
    large = jnp.minimum(large, half - 1)
    return bucket + jnp.where(n < max_exact, n, large)


def window_attention(q, k, v, sink, rel_bias):
    B, S = q.shape[:2]
    nb = S // BLOCK
    G = N_Q_HEADS // N_KV_HEADS
    qb = q.reshape(B, nb, BLOCK, N_KV_HEADS, G, HEAD_DIM_ATT)

    def bands(t):
        t = t.reshape(B, S, N_KV_HEADS, HEAD_DIM_ATT)
        tp = jnp.pad(t, ((0, 0), (BLOCK, BLOCK), (0, 0), (0, 0)))
        tb = tp.reshape(B, nb + 2, BLOCK, N_KV_HEADS, HEAD_DIM_ATT)
        return jnp.concatenate([tb[:, :-2], tb[:, 1:-1], tb[:, 2:]], axis=2)

    kb, vb = bands(k), bands(v)
    scores = jnp.einsum('bnqhgd,bnkhd->bnhgqk', qb, kb,
                        preferred_element_type=jnp.float32) * (HEAD_DIM_ATT ** -0.5)
    qi = jnp.arange(BLOCK)[:, None]
    kj = jnp.arange(3 * BLOCK)[None, :]
    rel = kj - BLOCK - qi
    bias = rel_bias.astype(jnp.float32)[t5_bucket(rel)]
    bias = bias.transpose(2, 0, 1).reshape(N_KV_HEADS, G, BLOCK, 3 * BLOCK)
    key_pos = jnp.arange(nb)[:, None] * BLOCK - BLOCK + kj
    in_range = (key_pos >= 0) & (key_pos < S)
    valid = (jnp.abs(rel) <= WINDOW)[None, :, :] & in_range[:, None, :]
    scores = jnp.where(valid[None, :, None, None], scores + bias, NEG_INF)
    s = sink.astype(jnp.float32).reshape(N_KV_HEADS, G, 1, 1)
    m = jnp.maximum(jnp.max(scores, axis=-1, keepdims=True), s)
    p = jnp.exp(scores - m)
    denom = jnp.sum(p, axis=-1, keepdims=True) + jnp.exp(s - m)
    out = jnp.einsum('bnhgqk,bnkhd->bnqhgd', (p / denom).astype(v.dtype), vb)
    return out.reshape(B, S, ATT_WIDTH)


def rotary_ret(x):
    S = x.shape[1]
    half = RET_HEAD_DIM // 2
    inv = 1.0 / (10000.0 ** jnp.linspace(0.0, 1.0, half, dtype=jnp.float32))
    ang = jnp.arange(S, dtype=jnp.float32)[:, None] * inv[None, :]
    cos = jnp.cos(ang)[None, :, None, :]
    sin = jnp.sin(ang)[None, :, None, :]
    x1, x2 = x[..., :half], x[..., half:]
    return jnp.concatenate([x1 * cos - x2 * sin, x1 * sin + x2 * cos], axis=-1)


def retention_one_dir(q, k, v, log_gamma, include_diag):
    B, S = q.shape[:2]
    nc = S // CHUNK
    qc = q.reshape(B, nc, CHUNK, N_RET_HEADS, RET_HEAD_DIM)
    kc = k.reshape(B, nc, CHUNK, N_RET_HEADS, RET_HEAD_DIM)
    vc = v.reshape(B, nc, CHUNK, N_RET_HEADS, RET_HEAD_DIM)
    idx = jnp.arange(CHUNK, dtype=jnp.float32)
    diff = idx[:, None] - idx[None, :]
    mask = (diff >= 0) if include_diag else (diff > 0)
    decay_intra = jnp.where(mask[None], jnp.exp(log_gamma[:, None, None] * jnp.maximum(diff, 0.0)[None]), 0.0)
    intra = jnp.einsum('bnihd,bnjhd->bnhij', qc, kc) * decay_intra[None, None]
    o_intra = jnp.einsum('bnhij,bnjhe->bnihe', intra, vc)
    k_dec = jnp.exp(log_gamma[None, :] * (CHUNK - 1 - idx)[:, None])
    kv = jnp.einsum('bnjhd,bnjhe->nbhde', kc * k_dec[None, None, :, :, None], vc)
    g_chunk = jnp.exp(log_gamma * CHUNK)[None, :, None, None]

    def step(R, kv_c):
        return g_chunk * R + kv_c, R

    _, r_prev = lax.scan(step, jnp.zeros_like(kv[0]), kv)
    q_dec = jnp.exp(log_gamma[None, :] * (idx + 1.0)[:, None])
    o_cross = jnp.einsum('bnihd,nbhde->bnihe', qc * q_dec[None, None, :, :, None], r_prev)
    return (o_intra + o_cross).reshape(B, S, N_RET_HEADS, RET_HEAD_DIM)


def bidirectional_retention(q, k, v, gate, decay_f, decay_b):
    B, S = q.shape[:2]
    shp = (B, S, N_RET_HEADS, RET_HEAD_DIM)
    qf = rotary_ret(q.reshape(shp).astype(jnp.float32))
    kf = rotary_ret(k.reshape(shp).astype(jnp.float32)) * (RET_HEAD_DIM ** -0.5)
    vf = v.reshape(shp).astype(jnp.float32)
    ld_f = jax.nn.log_sigmoid(decay_f.astype(jnp.float32))
    ld_b = jax.nn.log_sigmoid(decay_b.astype(jnp.float32))
    o_f = retention_one_dir(qf, kf, vf, ld_f, True)
    o_b = retention_one_dir(qf[:, ::-1], kf[:, ::-1], vf[:, ::-1], ld_b, False)[:, ::-1]
    o = o_f + o_b
    mu = jnp.mean(o, axis=-1, keepdims=True)
    var = jnp.mean(jnp.square(o - mu), axis=-1, keepdims=True)
    o = ((o - mu) * lax.rsqrt(var + EPS)).reshape(B, S, RET_WIDTH)
    return (o * jax.nn.silu(gate.astype(jnp.float32))).astype(q.dtype)


def expert_choice_ffn(x, w_router, w_gate, w_up, w_down):
    B, S, D = x.shape
    n = B * S
    cap = CAPACITY_FACTOR * n // N_EXPERTS
    xt = x.reshape(n, D)
    affinity = jax.nn.softmax(jnp.matmul(xt, w_router).astype(jnp.float32), axis=-1)
    gates, idx = lax.top_k(affinity.T, cap)
    xe = xt[idx]
    h = jax.nn.silu(jnp.einsum('ecd,edf->ecf', xe, w_gate)) * jnp.einsum('ecd,edf->ecf', xe, w_up)
    ye = jnp.einsum('ecf,efd->ecd', h, w_down) * gates[..., None].astype(x.dtype)
    out = jnp.zeros_like(xt).at[idx.reshape(-1)].add(ye.reshape(-1, D))
    return out.reshape(B, S, D)


def encoder_layer(x, norm_mix_g, w_in, attn_sink, rel_bias, ret_decay_fwd, ret_decay_bwd,
                  w_out, norm_ffn_g, w_router, w_gate, w_up, w_down):
    h = rms_norm(x, norm_mix_g)
    proj = jnp.matmul(h, w_in)
    q_a, k_a, v_a, q_r, k_r, v_r, g_r = jnp.split(proj, IN_OFFSETS, axis=-1)
    att = window_attention(q_a, k_a, v_a, attn_sink, rel_bias)
    ret = bidirectional_retention(q_r, k_r, v_r, g_r, ret_decay_fwd, ret_decay_bwd)
    x = x + jnp.matmul(jnp.concatenate([att, ret], axis=-1), w_out)
    x = x + expert_choice_ffn(rms_norm(x, norm_ffn_g), w_router, w_gate, w_up, w_down)
    return x


def setup_inputs(seed: int = 0) -> dict:
    key = jax.random.key(seed)
    ks = jax.random.split(key, 16)
    f32 = jnp.float32
    x_prompt = jax.random.normal(ks[0], (BATCH, SEQ, D_MODEL), f32)
    x_sample = jax.random.normal(ks[1], (DEC_BATCH, DEC_SEQ, D_MODEL), f32)
    norm_mix_g = 1.0 + 0.01 * jax.random.normal(ks[2], (DEPTH, D_MODEL), f32)
    w_in = jax.random.normal(ks[3], (DEPTH, D_MODEL, IN_WIDTH), f32) * D_MODEL ** -0.5
    attn_sink = 0.5 * jax.random.normal(ks[4], (DEPTH, N_Q_HEADS), f32)
    rel_bias = 0.5 * jax.random.normal(ks[5], (N_BUCKETS, N_Q_HEADS), f32)
    e = jnp.linspace(5.0, 12.0, N_RET_HEADS, dtype=f32)
    base = jnp.log(2.0 ** e - 1.0)
    ret_decay_fwd = base[None, :] + 0.05 * jax.random.normal(ks[6], (DEPTH, N_RET_HEADS), f32)
    ret_decay_bwd = base[None, :] + 0.05 * jax.random.normal(ks[7], (DEPTH, N_RET_HEADS), f32)
    w_out = jax.random.normal(ks[8], (DEPTH, D_MIX, D_MODEL), f32) * D_MIX ** -0.5
    norm_ffn_g = 1.0 + 0.01 * jax.random.normal(ks[9], (DEPTH, D_MODEL), f32)
    w_router = jax.random.normal(ks[10], (DEPTH, D_MODEL, N_EXPERTS), f32) * D_MODEL ** -0.5
    w_gate = jax.random.normal(ks[11], (DEPTH, N_EXPERTS, D_MODEL, EXPERT_FF), f32) * D_MODEL ** -0.5
    w_up = jax.random.normal(ks[12], (DEPTH, N_EXPERTS, D_MODEL, EXPERT_FF), f32) * D_MODEL ** -0.5
    w_down = jax.random.normal(ks[13], (DEPTH, N_EXPERTS, EXPERT_FF, D_MODEL), f32) * EXPERT_FF ** -0.5
    norm_final_g = 1.0 + 0.01 * jax.random.normal(ks[14], (D_MODEL,), f32)
    return {"x_prompt": x_prompt, "x_sample": x_sample, "norm_mix_g": norm_mix_g, "w_in": w_in,
            "attn_sink": attn_sink, "rel_bias": rel_bias, "ret_decay_fwd": ret_decay_fwd,
            "ret_decay_bwd": ret_decay_bwd, "w_out": w_out, "norm_ffn_g": norm_ffn_g,
            "w_router": w_router, "w_gate": w_gate, "w_up": w_up, "w_down": w_down,
            "norm_final_g": norm_final_g}


def reference(x_prompt, x_sample, norm_mix_g, w_in, attn_sink, rel_bias, ret_decay_fwd,
              ret_decay_bwd, w_out, norm_ffn_g, w_router, w_gate, w_up, w_down, norm_final_g):
    def trunk(x):
        for l in range(DEPTH):
            x = encoder_layer(x, norm_mix_g[l], w_in[l], attn_sink[l], rel_bias,
                              ret_decay_fwd[l], ret_decay_bwd[l], w_out[l], norm_ffn_g[l],
                              w_router[l], w_gate[l], w_up[l], w_down[l])
        return rms_norm(x, norm_final_g)

    y_prompt = trunk(x_prompt)
    y_sample = trunk(x_sample)
    return (y_prompt, y_sample)
```

```python
import functools
import math

import jax
import jax.numpy as jnp
import numpy as np
from jax import lax
from jax.experimental import pallas as pl
from jax.experimental.pallas import tpu as pltpu

D_MODEL = 2048
HEAD_DIM_ATT = 128
N_Q_HEADS = 8
N_KV_HEADS = 2
GQA = N_Q_HEADS // N_KV_HEADS
ATT_WIDTH = N_Q_HEADS * HEAD_DIM_ATT
KV_WIDTH = N_KV_HEADS * HEAD_DIM_ATT
WINDOW = 128
BLOCK = 128
N_BUCKETS = 32
MAX_DISTANCE = 128
N_RET_HEADS = 4
RET_HEAD_DIM = 256
RET_WIDTH = N_RET_HEADS * RET_HEAD_DIM
CHUNK = 128
D_MIX = ATT_WIDTH + RET_WIDTH
IN_WIDTH = ATT_WIDTH + 2 * KV_WIDTH + 4 * RET_WIDTH
N_EXPERTS = 16
CAPACITY_FACTOR = 2
EXPERT_FF = D_MODEL // 2
EPS = 1e-6
NEG_INF = -1e30
LANES = 128

VMEM_LIMIT_BYTES = 56 * 1024 * 1024

ROW_TILE_INPROJ = 1024
COL_TILE_INPROJ = 512
ROW_TILE_OUTPROJ = 512
ATT_Q_TILE = 512
RET_SEG = 2048
MOE_SLOT_TILE = 512
NORM_ROW_TILE = 1024

BF16 = jnp.bfloat16
F32 = jnp.float32


def _params(*sem):
    return pltpu.CompilerParams(dimension_semantics=sem, vmem_limit_bytes=VMEM_LIMIT_BYTES)


def _norm_inproj_kernel(x_ref, g_ref, w_ref, o_ref, h_ref):
    @pl.when(pl.program_id(1) == 0)
    def _():
        x = x_ref[...]
        var = jnp.mean(x * x, axis=-1, keepdims=True)
        h_ref[...] = (x * lax.rsqrt(var + EPS) * g_ref[...]).astype(BF16)

    o_ref[...] = jnp.dot(h_ref[...], w_ref[...], preferred_element_type=F32).astype(o_ref.dtype)


def norm_inproj(x2d, g, w_bf16):
    n, d = x2d.shape
    width = w_bf16.shape[1]
    tm = min(ROW_TILE_INPROJ, n)
    tn = COL_TILE_INPROJ
    return pl.pallas_call(
        _norm_inproj_kernel,
        grid=(n // tm, width // tn),
        in_specs=[
            pl.BlockSpec((tm, d), lambda i, j: (i, 0)),
            pl.BlockSpec((1, d), lambda i, j: (0, 0)),
            pl.BlockSpec((d, tn), lambda i, j: (0, j)),
        ],
        out_specs=pl.BlockSpec((tm, tn), lambda i, j: (i, j)),
        out_shape=jax.ShapeDtypeStruct((n, width), BF16),
        scratch_shapes=[pltpu.VMEM((tm, d), BF16)],
        compiler_params=_params("parallel", "arbitrary"),
        name="norm_inproj",
    )(x2d, g.reshape(1, d), w_bf16)


def _t5_bucket(rel):
    half = N_BUCKETS // 2
    max_exact = half // 2
    bucket = jnp.where(rel > 0, half, 0)
    n = jnp.abs(rel)
    nf = jnp.maximum(n, 1).astype(F32)
    large = max_exact + (jnp.log(nf / max_exact) / math.log(MAX_DISTANCE / max_exact)
                         * (half - max_exact)).astype(jnp.int32)
    large = jnp.minimum(large, half - 1)
    return bucket + jnp.where(n < max_exact, n, large)


def _band_bias(rel_bias):
    qi = jnp.arange(BLOCK)[:, None]
    kj = jnp.arange(3 * BLOCK)[None, :]
    rel = kj - BLOCK - qi
    bias = rel_bias.astype(F32)[_t5_bucket(rel)].transpose(2, 0, 1)
    return jnp.where((jnp.abs(rel) <= WINDOW)[None], bias, NEG_INF)


def _attn_kernel(q_ref, kp_ref, km_ref, kn_ref, vp_ref, vm_ref, vn_ref, bias_ref, sink_ref,
                 o_ref, k_scr, v_scr, *, seq_len, tq):
    i = pl.program_id(1)
    k_scr[0:BLOCK] = kp_ref[0]
    k_scr[BLOCK:BLOCK + tq] = km_ref[0]
    k_scr[BLOCK + tq:] = kn_ref[0]
    v_scr[0:BLOCK] = vp_ref[0]
    v_scr[BLOCK:BLOCK + tq] = vm_ref[0]
    v_scr[BLOCK + tq:] = vn_ref[0]
    col = lax.broadcasted_iota(jnp.int32, (1, 3 * BLOCK), 1)
    scale = HEAD_DIM_ATT ** -0.5
    for sb in range(tq // BLOCK):
        key_pos = i * tq + (sb * BLOCK - BLOCK) + col
        in_range = (key_pos >= 0) & (key_pos < seq_len)
        for h in range(N_KV_HEADS):
            c0 = h * HEAD_DIM_ATT
            kb = k_scr[sb * BLOCK:(sb + 3) * BLOCK, c0:c0 + HEAD_DIM_ATT]
            vb = v_scr[sb * BLOCK:(sb + 3) * BLOCK, c0:c0 + HEAD_DIM_ATT]
            qs = jnp.concatenate(
                [q_ref[0, sb * BLOCK:(sb + 1) * BLOCK,
                       (h * GQA + g) * HEAD_DIM_ATT:(h * GQA + g + 1) * HEAD_DIM_ATT]
                 for g in range(GQA)], axis=0)
            s = lax.dot_general(qs, kb, (((1,), (1,)), ((), ())),
                                preferred_element_type=F32) * scale
            bias = bias_ref[h * GQA:(h + 1) * GQA].reshape(GQA * BLOCK, 3 * BLOCK)
            s = jnp.where(in_range, s + bias, NEG_INF)
            sink = jnp.concatenate(
                [jnp.broadcast_to(sink_ref[h * GQA + g:h * GQA + g + 1, 0:1], (BLOCK, 1))
                 for g in range(GQA)], axis=0)
            m = jnp.maximum(jnp.max(s, axis=-1, keepdims=True), sink)
            p = jnp.exp(s - m)
            denom = jnp.sum(p, axis=-1, keepdims=True) + jnp.exp(sink - m)
            o = jnp.dot(p.astype(BF16), vb, preferred_element_type=F32) / denom
            for g in range(GQA):
                o_ref[0, sb * BLOCK:(sb + 1) * BLOCK,
                      (h * GQA + g) * HEAD_DIM_ATT:(h * GQA + g + 1) * HEAD_DIM_ATT] = (
                    o[g * BLOCK:(g + 1) * BLOCK].astype(o_ref.dtype))


def window_attention(proj, band_bias, sink):
    b, s, _ = proj.shape
    tq = min(ATT_Q_TILE, s)
    nblk = s // BLOCK
    r = tq // BLOCK
    k_col = ATT_WIDTH // KV_WIDTH
    v_col = k_col + 1

    def main(col):
        return pl.BlockSpec((1, tq, KV_WIDTH), lambda bi, i: (bi, i, col))

    def prev(col):
        return pl.BlockSpec((1, BLOCK, KV_WIDTH), lambda bi, i: (bi, jnp.maximum(i * r - 1, 0), col))

    def nxt(col):
        return pl.BlockSpec((1, BLOCK, KV_WIDTH),
                            lambda bi, i: (bi, jnp.minimum((i + 1) * r, nblk - 1), col))

    sink_b = jnp.broadcast_to(sink.astype(F32)[:, None], (N_Q_HEADS, LANES))
    return pl.pallas_call(
        functools.partial(_attn_kernel, seq_len=s, tq=tq),
        grid=(b, s // tq),
        in_specs=[
            pl.BlockSpec((1, tq, ATT_WIDTH), lambda bi, i: (bi, i, 0)),
            prev(k_col), main(k_col), nxt(k_col),
            prev(v_col), main(v_col), nxt(v_col),
            pl.BlockSpec((N_Q_HEADS, BLOCK, 3 * BLOCK), lambda bi, i: (0, 0, 0)),
            pl.BlockSpec((N_Q_HEADS, LANES), lambda bi, i: (0, 0)),
        ],
        out_specs=pl.BlockSpec((1, tq, ATT_WIDTH), lambda bi, i: (bi, i, 0)),
        out_shape=jax.ShapeDtypeStruct((b, s, ATT_WIDTH), BF16),
        scratch_shapes=[pltpu.VMEM((tq + 2 * BLOCK, KV_WIDTH), BF16),
                        pltpu.VMEM((tq + 2 * BLOCK, KV_WIDTH), BF16)],
        compiler_params=_params("parallel", "arbitrary"),
        name="window_attention",
    )(proj, proj, proj, proj, proj, proj, proj, band_bias, sink_b)


def _rotate(x, cos, sin):
    half = RET_HEAD_DIM // 2
    x1 = x[:, :half]
    x2 = x[:, half:]
    return jnp.concatenate([x1 * cos - x2 * sin, x1 * sin + x2 * cos], axis=-1)


def _ret_kernel(ld_ref, q_ref, k_ref, v_ref, g_ref, cos_ref, sin_ref, o_ref, oacc, rf, rb,
                *, nseg, seg_len):
    h = pl.program_id(1)
    s = pl.program_id(2)
    nchunk = seg_len // CHUNK
    ld_f = ld_ref[0, h]
    ld_b = ld_ref[1, h]
    ii = lax.broadcasted_iota(jnp.int32, (CHUNK, 1), 0).astype(F32)
    jj = lax.broadcasted_iota(jnp.int32, (1, CHUNK), 1).astype(F32)
    k_scale = RET_HEAD_DIM ** -0.5
    tdims = (((0,), (0,)), ((), ()))

    def load(r0):
        cos = cos_ref[pl.ds(r0, CHUNK), :]
        sin = sin_ref[pl.ds(r0, CHUNK), :]
        q = _rotate(q_ref[0, pl.ds(r0, CHUNK), :].astype(F32), cos, sin)
        k = _rotate(k_ref[0, pl.ds(r0, CHUNK), :].astype(F32), cos, sin) * k_scale
        return q, k, v_ref[0, pl.ds(r0, CHUNK), :]

    @pl.when(s == 0)
    def _():
        rf[...] = jnp.zeros_like(rf)
        rb[...] = jnp.zeros_like(rb)

    @pl.when(s < nseg)
    def _():
        diff = ii - jj
        dmat = jnp.where(diff >= 0, jnp.exp(ld_f * jnp.maximum(diff, 0.0)),
                         jnp.exp(ld_b * jnp.maximum(-diff, 0.0)))
        q_dec = jnp.exp(ld_f * (ii + 1.0))
        k_dec = jnp.exp(ld_f * (CHUNK - 1.0 - ii))
        g_chunk = jnp.exp(jnp.full((1, 1), CHUNK, F32) * ld_f)

        def body(c, carry):
            r0 = pl.multiple_of(c * CHUNK, CHUNK)
            q, k, v = load(r0)
            a = lax.dot_general(q.astype(BF16), k.astype(BF16), (((1,), (1,)), ((), ())),
                                preferred_element_type=F32) * dmat
            o = jnp.dot(a.astype(BF16), v, preferred_element_type=F32)
            o = o + jnp.dot((q * q_dec).astype(BF16), rf[...].astype(BF16),
                            preferred_element_type=F32)
            oacc[pl.ds(pl.multiple_of(s * seg_len + r0, CHUNK), CHUNK), :] = o
            kv = lax.dot_general((k * k_dec).astype(BF16), v, tdims, preferred_element_type=F32)
            rf[...] = g_chunk * rf[...] + kv
            return carry

        lax.fori_loop(0, nchunk, body, 0)

    @pl.when(s >= nseg)
    def _():
        seg = 2 * nseg - 1 - s
        q_dec = jnp.exp(ld_b * (CHUNK - ii))
        k_dec = jnp.exp(ld_b * ii)
        g_chunk = jnp.exp(jnp.full((1, 1), CHUNK, F32) * ld_b)

        def body(t, carry):
            r0 = pl.multiple_of((nchunk - 1 - t) * CHUNK, CHUNK)
            q, k, v = load(r0)
            o = oacc[pl.ds(pl.multiple_of(seg * seg_len + r0, CHUNK), CHUNK), :]
            o = o + jnp.dot((q * q_dec).astype(BF16), rb[...].astype(BF16),
                            preferred_element_type=F32)
            mu = jnp.mean(o, axis=-1, keepdims=True)
            oc = o - mu
            var = jnp.mean(oc * oc, axis=-1, keepdims=True)
            gate = g_ref[0, pl.ds(r0, CHUNK), :].astype(F32)
            silu = gate * (1.0 / (1.0 + jnp.exp(-gate)))
            o_ref[0, pl.ds(r0, CHUNK), :] = (oc * lax.rsqrt(var + EPS) * silu).astype(o_ref.dtype)
            kv = lax.dot_general((k * k_dec).astype(BF16), v, tdims, preferred_element_type=F32)
            rb[...] = g_chunk * rb[...] + kv
            return carry

        lax.fori_loop(0, nchunk, body, 0)


def _rotary_tables(seq_len):
    half = RET_HEAD_DIM // 2
    inv = 1.0 / (10000.0 ** jnp.linspace(0.0, 1.0, half, dtype=F32))
    ang = jnp.arange(seq_len, dtype=F32)[:, None] * inv[None, :]
    return jnp.cos(ang), jnp.sin(ang)


def retention(proj, log_decay, cos, sin):
    b, s, _ = proj.shape
    seg_len = min(RET_SEG, s)
    nseg = s // seg_len
    base = (ATT_WIDTH + 2 * KV_WIDTH) // RET_HEAD_DIM

    def seg_in(t):
        return jnp.where(t < nseg, t, 2 * nseg - 1 - t)

    def seg_out(t):
        return jnp.where(t < nseg, nseg - 1, 2 * nseg - 1 - t)

    def head_block(which):
        off = base + which * N_RET_HEADS
        return pl.BlockSpec((1, seg_len, RET_HEAD_DIM), lambda bi, h, t: (bi, seg_in(t), off + h))

    table = pl.BlockSpec((seg_len, RET_HEAD_DIM // 2), lambda bi, h, t: (seg_in(t), 0))
    return pl.pallas_call(
        functools.partial(_ret_kernel, nseg=nseg, seg_len=seg_len),
        grid=(b, N_RET_HEADS, 2 * nseg),
        in_specs=[pl.BlockSpec(memory_space=pltpu.SMEM),
                  head_block(0), head_block(1), head_block(2), head_block(3), table, table],
        out_specs=pl.BlockSpec((1, seg_len, RET_HEAD_DIM), lambda bi, h, t: (bi, seg_out(t), h)),
        out_shape=jax.ShapeDtypeStruct((b, s, RET_WIDTH), BF16),
        scratch_shapes=[pltpu.VMEM((s, RET_HEAD_DIM), F32),
                        pltpu.VMEM((RET_HEAD_DIM, RET_HEAD_DIM), F32),
                        pltpu.VMEM((RET_HEAD_DIM, RET_HEAD_DIM), F32)],
        compiler_params=_params("parallel", "arbitrary", "arbitrary"),
        name="retention",
    )(log_decay, proj, proj, proj, proj, cos, sin)


def _outproj_kernel(att_ref, ret_ref, x_ref, wa_ref, wr_ref, g_ref, wrt_ref, xo_ref, hn_ref, aff_ref):
    y = x_ref[...] + jnp.dot(att_ref[...], wa_ref[...], preferred_element_type=F32)
    y = y + jnp.dot(ret_ref[...], wr_ref[...], preferred_element_type=F32)
    xo_ref[...] = y
    var = jnp.mean(y * y, axis=-1, keepdims=True)
    hn = (y * lax.rsqrt(var + EPS) * g_ref[...]).astype(BF16)
    hn_ref[...] = hn
    logits = jnp.dot(hn, wrt_ref[...], preferred_element_type=F32)
    lane = lax.broadcasted_iota(jnp.int32, (1, LANES), 1)
    logits = jnp.where(lane < N_EXPERTS, logits, NEG_INF)
    e = jnp.exp(logits - jnp.max(logits, axis=-1, keepdims=True))
    aff_ref[...] = e / jnp.sum(e, axis=-1, keepdims=True)


def outproj_norm_router(att, ret, x2d, w_out_bf16, g, w_router_pad):
    n, d = x2d.shape
    tm = min(ROW_TILE_OUTPROJ, n)
    row = lambda i: (i, 0)
    return pl.pallas_call(
        _outproj_kernel,
        grid=(n // tm,),
        in_specs=[
            pl.BlockSpec((tm, ATT_WIDTH), row),
            pl.BlockSpec((tm, RET_WIDTH), row),
            pl.BlockSpec((tm, d), row),
            pl.BlockSpec((ATT_WIDTH, d), lambda i: (0, 0)),
            pl.BlockSpec((RET_WIDTH, d), lambda i: (1, 0)),
            pl.BlockSpec((1, d), lambda i: (0, 0)),
            pl.BlockSpec((d, LANES), lambda i: (0, 0)),
        ],
        out_specs=[pl.BlockSpec((tm, d), row), pl.BlockSpec((tm, d), row),
                   pl.BlockSpec((tm, LANES), row)],
        out_shape=[jax.ShapeDtypeStruct((n, d), F32), jax.ShapeDtypeStruct((n, d), BF16),
                   jax.ShapeDtypeStruct((n, LANES), F32)],
        compiler_params=_params("parallel"),
        name="outproj_norm_router",
    )(att, ret, x2d, w_out_bf16, w_out_bf16, g.reshape(1, d), w_router_pad)


def _moe_kernel(xe_ref, wg_ref, wu_ref, wd_ref, gate_ref, o_ref):
    x = xe_ref[0]
    hg = jnp.dot(x, wg_ref[0], preferred_element_type=F32)
    hu = jnp.dot(x, wu_ref[0], preferred_element_type=F32)
    h = hg * (1.0 / (1.0 + jnp.exp(-hg))) * hu
    y = jnp.dot(h.astype(BF16), wd_ref[0], preferred_element_type=F32)
    o_ref[0] = y * gate_ref[0]


def expert_ffn(xe, gates, wg, wu, wd):
    e, cap, d = xe.shape
    ff = wg.shape[-1]
    tc = min(MOE_SLOT_TILE, cap)
    return pl.pallas_call(
        _moe_kernel,
        grid=(e, cap // tc),
        in_specs=[
            pl.BlockSpec((1, tc, d), lambda ei, t: (ei, t, 0)),
            pl.BlockSpec((1, d, ff), lambda ei, t: (ei, 0, 0)),
            pl.BlockSpec((1, d, ff), lambda ei, t: (ei, 0, 0)),
            pl.BlockSpec((1, ff, d), lambda ei, t: (ei, 0, 0)),
            pl.BlockSpec((1, tc, 1), lambda ei, t: (ei, t, 0)),
        ],
        out_specs=pl.BlockSpec((1, tc, d), lambda ei, t: (ei, t, 0)),
        out_shape=jax.ShapeDtypeStruct((e, cap, d), F32),
        compiler_params=_params("parallel", "arbitrary"),
        name="expert_ffn",
    )(xe, wg, wu, wd, gates)


def _norm_kernel(x_ref, g_ref, o_ref):
    x = x_ref[...]
    var = jnp.mean(x * x, axis=-1, keepdims=True)
    o_ref[...] = x * lax.rsqrt(var + EPS) * g_ref[...]


def final_norm(x2d, g):
    n, d = x2d.shape
    tm = min(NORM_ROW_TILE, n)
    return pl.pallas_call(
        _norm_kernel,
        grid=(n // tm,),
        in_specs=[pl.BlockSpec((tm, d), lambda i: (i, 0)), pl.BlockSpec((1, d), lambda i: (0, 0))],
        out_specs=pl.BlockSpec((tm, d), lambda i: (i, 0)),
        out_shape=jax.ShapeDtypeStruct((n, d), F32),
        compiler_params=_params("parallel"),
        name="final_norm",
    )(x2d, g.reshape(1, d))


def _layer(x2d, bsz, seq, lw, band_bias, cos, sin):
    n = bsz * seq
    proj = norm_inproj(x2d, lw["norm_mix_g"], lw["w_in"]).reshape(bsz, seq, IN_WIDTH)
    att = window_attention(proj, band_bias, lw["attn_sink"]).reshape(n, ATT_WIDTH)
    ret = retention(proj, lw["log_decay"], cos, sin).reshape(n, RET_WIDTH)
    x2d, hn, aff = outproj_norm_router(att, ret, x2d, lw["w_out"], lw["norm_ffn_g"], lw["w_router"])
    cap = CAPACITY_FACTOR * n // N_EXPERTS
    gates, idx = lax.top_k(aff[:, :N_EXPERTS].T, cap)
    ye = expert_ffn(hn[idx], gates[..., None], lw["w_gate"], lw["w_up"], lw["w_down"])
    return x2d.at[idx.reshape(-1)].add(ye.reshape(-1, D_MODEL))


def kernel(x_prompt, x_sample, norm_mix_g, w_in, attn_sink, rel_bias, ret_decay_fwd, ret_decay_bwd,
           w_out, norm_ffn_g, w_router, w_gate, w_up, w_down, norm_final_g):
    depth = w_in.shape[0]
    band_bias = _band_bias(rel_bias)
    layers = []
    for l in range(depth):
        layers.append({
            "norm_mix_g": norm_mix_g[l],
            "w_in": w_in[l].astype(BF16),
            "attn_sink": attn_sink[l],
            "log_decay": jnp.stack([jax.nn.log_sigmoid(ret_decay_fwd[l].astype(F32)),
                                    jax.nn.log_sigmoid(ret_decay_bwd[l].astype(F32))]),
            "w_out": w_out[l].astype(BF16),
            "norm_ffn_g": norm_ffn_g[l],
            "w_router": jnp.pad(w_router[l].astype(BF16), ((0, 0), (0, LANES - N_EXPERTS))),
            "w_gate": w_gate[l].astype(BF16),
            "w_up": w_up[l].astype(BF16),
            "w_down": w_down[l].astype(BF16),
        })

    def trunk(x):
        bsz, seq, d = x.shape
        cos, sin = _rotary_tables(seq)
        x2d = x.reshape(bsz * seq, d)
        for lw in layers:
            x2d = _layer(x2d, bsz, seq, lw, band_bias, cos, sin)
        return final_norm(x2d, norm_final_g).reshape(bsz, seq, d)

    return (trunk(x_prompt), trunk(x_sample))
```

```python
import functools
import math

import jax
import jax.numpy as jnp
import numpy as np
from jax import lax
from jax.experimental import pallas as pl
from jax.experimental.pallas import tpu as pltpu

D_MODEL = 2048
HEAD_DIM_ATT = 128
N_Q_HEADS = 8
N_KV_HEADS = 2
GQA = N_Q_HEADS // N_KV_HEADS
ATT_WIDTH = N_Q_HEADS * HEAD_DIM_ATT
KV_WIDTH = N_KV_HEADS * HEAD_DIM_ATT
WINDOW = 128
BLOCK = 128
N_BUCKETS = 32
MAX_DISTANCE = 128
N_RET_HEADS = 4
RET_HEAD_DIM = 256
RET_WIDTH = N_RET_HEADS * RET_HEAD_DIM
RET_CHUNK = 256
D_MIX = ATT_WIDTH + RET_WIDTH
IN_WIDTH = ATT_WIDTH + 2 * KV_WIDTH + 4 * RET_WIDTH
N_EXPERTS = 16
CAPACITY_FACTOR = 2
EXPERT_FF = D_MODEL // 2
EPS = 1e-6
NEG_INF = -1e30
LANES = 128

VMEM_LIMIT_BYTES = 56 * 1024 * 1024

ROW_TILE_INPROJ = 1024
COL_TILE_INPROJ = 512
ROW_TILE_OUTPROJ = 512
ATT_Q_TILE = 512
RET_SEG = 2048
MOE_SLOT_TILE = 512
NORM_ROW_TILE = 1024

BF16 = jnp.bfloat16
F32 = jnp.float32


def _params(*sem):
    return pltpu.CompilerParams(dimension_semantics=sem, vmem_limit_bytes=VMEM_LIMIT_BYTES)


def _norm_inproj_kernel(x_ref, g_ref, w_ref, o_ref, h_ref):
    @pl.when(pl.program_id(1) == 0)
    def _():
        x = x_ref[...]
        var = jnp.mean(x * x, axis=-1, keepdims=True)
        h_ref[...] = (x * lax.rsqrt(var + EPS) * g_ref[...]).astype(BF16)

    o_ref[...] = jnp.dot(h_ref[...], w_ref[...], preferred_element_type=F32).astype(o_ref.dtype)


def norm_inproj(x2d, g, w_bf16):
    n, d = x2d.shape
    width = w_bf16.shape[1]
    tm = min(ROW_TILE_INPROJ, n)
    tn = COL_TILE_INPROJ
    return pl.pallas_call(
        _norm_inproj_kernel,
        grid=(n // tm, width // tn),
        in_specs=[
            pl.BlockSpec((tm, d), lambda i, j: (i, 0)),
            pl.BlockSpec((1, d), lambda i, j: (0, 0)),
            pl.BlockSpec((d, tn), lambda i, j: (0, j)),
        ],
        out_specs=pl.BlockSpec((tm, tn), lambda i, j: (i, j)),
        out_shape=jax.ShapeDtypeStruct((n, width), BF16),
        scratch_shapes=[pltpu.VMEM((tm, d), BF16)],
        compiler_params=_params("parallel", "arbitrary"),
        name="norm_inproj",
    )(x2d, g.reshape(1, d), w_bf16)


def _t5_bucket(rel):
    half = N_BUCKETS // 2
    max_exact = half // 2
    bucket = jnp.where(rel > 0, half, 0)
    n = jnp.abs(rel)
    nf = jnp.maximum(n, 1).astype(F32)
    large = max_exact + (jnp.log(nf / max_exact) / math.log(MAX_DISTANCE / max_exact)
                         * (half - max_exact)).astype(jnp.int32)
    large = jnp.minimum(large, half - 1)
    return bucket + jnp.where(n < max_exact, n, large)


def _band_bias(rel_bias):
    qi = jnp.arange(BLOCK)[:, None]
    kj = jnp.arange(3 * BLOCK)[None, :]
    rel = kj - BLOCK - qi
    bias = rel_bias.astype(F32)[_t5_bucket(rel)]
    bias = jnp.where((jnp.abs(rel) <= WINDOW)[:, :, None], bias, NEG_INF)
    bias = bias.reshape(BLOCK, 3, BLOCK, N_KV_HEADS, GQA).transpose(3, 1, 2, 4, 0)
    bias = bias.reshape(N_KV_HEADS, 3, BLOCK, GQA * BLOCK)
    return jnp.concatenate([bias, jnp.full_like(bias[:, :1], NEG_INF)], axis=1)


def _attn_kernel(q_ref, kp_ref, km_ref, kn_ref, vp_ref, vm_ref, vn_ref, bias_ref, sink_ref,
                 o_ref, k_scr, v_scr, *, tq):
    i = pl.program_id(1)
    last = pl.num_programs(1) - 1
    k_scr[0:BLOCK] = kp_ref[0]
    k_scr[BLOCK:BLOCK + tq] = km_ref[0]
    k_scr[BLOCK + tq:] = kn_ref[0]
    v_scr[0:BLOCK] = vp_ref[0]
    v_scr[BLOCK:BLOCK + tq] = vm_ref[0]
    v_scr[BLOCK + tq:] = vn_ref[0]
    scale = HEAD_DIM_ATT ** -0.5
    nsb = tq // BLOCK
    nt = (((1,), (1,)), ((), ()))
    tn = (((0,), (0,)), ((), ()))
    for sb in range(nsb):
        for h in range(N_KV_HEADS):
            c0 = h * HEAD_DIM_ATT
            qs = jnp.concatenate(
                [q_ref[0, sb * BLOCK:(sb + 1) * BLOCK,
                       (h * GQA + g) * HEAD_DIM_ATT:(h * GQA + g + 1) * HEAD_DIM_ATT]
                 for g in range(GQA)], axis=0)
            parts = []
            for c in range(3):
                kc = k_scr[(sb + c) * BLOCK:(sb + c + 1) * BLOCK, c0:c0 + HEAD_DIM_ATT]
                if c == 0 and sb == 0:
                    slot = jnp.where(i == 0, 3, 0)
                elif c == 2 and sb == nsb - 1:
                    slot = jnp.where(i == last, 3, 2)
                else:
                    slot = c
                s_c = lax.dot_general(kc, qs, nt, preferred_element_type=F32)
                parts.append(s_c * scale + bias_ref[h, slot])
            s = jnp.concatenate(parts, axis=0)
            sink = sink_ref[h:h + 1, :]
            m = jnp.maximum(jnp.max(s, axis=0, keepdims=True), sink)
            p = jnp.exp(s - m)
            denom = jnp.sum(p, axis=0, keepdims=True) + jnp.exp(sink - m)
            pn = (p * (1.0 / denom)).astype(BF16)
            vb = v_scr[sb * BLOCK:(sb + 3) * BLOCK, c0:c0 + HEAD_DIM_ATT]
            o = lax.dot_general(pn, vb, tn, preferred_element_type=F32)
            for g in range(GQA):
                o_ref[0, sb * BLOCK:(sb + 1) * BLOCK,
                      (h * GQA + g) * HEAD_DIM_ATT:(h * GQA + g + 1) * HEAD_DIM_ATT] = (
                    o[g * BLOCK:(g + 1) * BLOCK].astype(o_ref.dtype))


def window_attention(proj, band_bias, sink):
    b, s, _ = proj.shape
    tq = min(ATT_Q_TILE, s)
    nblk = s // BLOCK
    r = tq // BLOCK
    assert r >= 2 and s % tq == 0
    k_col = ATT_WIDTH // KV_WIDTH
    v_col = k_col + 1

    def main(col):
        return pl.BlockSpec((1, tq, KV_WIDTH), lambda bi, i: (bi, i, col))

    def prev(col):
        return pl.BlockSpec((1, BLOCK, KV_WIDTH), lambda bi, i: (bi, jnp.maximum(i * r - 1, 0), col))

    def nxt(col):
        return pl.BlockSpec((1, BLOCK, KV_WIDTH),
                            lambda bi, i: (bi, jnp.minimum((i + 1) * r, nblk - 1), col))

    sink_rows = jnp.repeat(sink.astype(F32), BLOCK).reshape(N_KV_HEADS, GQA * BLOCK)
    return pl.pallas_call(
        functools.partial(_attn_kernel, tq=tq),
        grid=(b, s // tq),
        in_specs=[
            pl.BlockSpec((1, tq, ATT_WIDTH), lambda bi, i: (bi, i, 0)),
            prev(k_col), main(k_col), nxt(k_col),
            prev(v_col), main(v_col), nxt(v_col),
            pl.BlockSpec((N_KV_HEADS, 4, BLOCK, GQA * BLOCK), lambda bi, i: (0, 0, 0, 0)),
            pl.BlockSpec((N_KV_HEADS, GQA * BLOCK), lambda bi, i: (0, 0)),
        ],
        out_specs=pl.BlockSpec((1, tq, ATT_WIDTH), lambda bi, i: (bi, i, 0)),
        out_shape=jax.ShapeDtypeStruct((b, s, ATT_WIDTH), BF16),
        scratch_shapes=[pltpu.VMEM((tq + 2 * BLOCK, KV_WIDTH), BF16),
                        pltpu.VMEM((tq + 2 * BLOCK, KV_WIDTH), BF16)],
        compiler_params=_params("parallel", "arbitrary"),
        name="window_attention",
    )(proj, proj, proj, proj, proj, proj, proj, band_bias, sink_rows)


def _rotate(x, cos, sin):
    half = RET_HEAD_DIM // 2
    x1 = x[:, :half]
    x2 = x[:, half:]
    return jnp.concatenate([x1 * cos - x2 * sin, x1 * sin + x2 * cos], axis=-1)


def _ret_kernel(ld_ref, q_ref, k_ref, v_ref, g_ref, cos_ref, sin_ref, o_ref,
                oacc, qrot, kbwd, kv_scr, rprev, rstate, *, nseg, seg_len):
    h = pl.program_id(1)
    s = pl.program_id(2)
    cs = RET_CHUNK
    nchunk = seg_len // cs
    ld_f = ld_ref[0, h]
    ld_b = ld_ref[1, h]
    ii = lax.broadcasted_iota(jnp.int32, (cs, 1), 0).astype(F32)
    jj = lax.broadcasted_iota(jnp.int32, (1, cs), 1).astype(F32)
    k_scale = RET_HEAD_DIM ** -0.5
    nt = (((1,), (1,)), ((), ()))
    tn = (((0,), (0,)), ((), ()))
    chunk_len = jnp.full((1, 1), cs, F32)

    @pl.when(s == 0)
    def _():
        rstate[...] = jnp.zeros_like(rstate)

    @pl.when(s < nseg)
    def _():
        base = s * seg_len
        diff = ii - jj
        dmat = jnp.where(diff >= 0, jnp.exp(ld_f * jnp.maximum(diff, 0.0)),
                         jnp.exp(ld_b * jnp.maximum(-diff, 0.0)))
        q_dec = jnp.exp(ld_f * (ii + 1.0))
        k_dec_f = jnp.exp(ld_f * (cs - 1.0 - ii))
        k_dec_b = jnp.exp(ld_b * ii)
        g_chunk = jnp.exp(chunk_len * ld_f)

        def products(c, carry):
            r0 = pl.multiple_of(c * cs, cs)
            rows = pl.ds(pl.multiple_of(base + r0, cs), cs)
            cos = cos_ref[pl.ds(r0, cs), :]
            sin = sin_ref[pl.ds(r0, cs), :]
            q = _rotate(q_ref[0, pl.ds(r0, cs), :].astype(F32), cos, sin).astype(BF16)
            k = _rotate(k_ref[0, pl.ds(r0, cs), :].astype(F32), cos, sin) * k_scale
            v = v_ref[0, pl.ds(r0, cs), :]
            qrot[rows, :] = q
            kbwd[rows, :] = (k * k_dec_b).astype(BF16)
            a = lax.dot_general(q, k.astype(BF16), nt, preferred_element_type=F32) * dmat
            oacc[rows, :] = jnp.dot(a.astype(BF16), v, preferred_element_type=F32)
            kv_scr[c] = lax.dot_general((k * k_dec_f).astype(BF16), v, tn, preferred_element_type=F32)
            return carry

        lax.fori_loop(0, nchunk, products, 0)

        def scan(c, carry):
            r = rstate[0]
            rprev[c] = r.astype(BF16)
            rstate[0] = g_chunk * r + kv_scr[c]
            return carry

        lax.fori_loop(0, nchunk, scan, 0)

        def cross(c, carry):
            rows = pl.ds(pl.multiple_of(base + c * cs, cs), cs)
            oacc[rows, :] += q_dec * jnp.dot(qrot[rows, :], rprev[c], preferred_element_type=F32)
            return carry

        lax.fori_loop(0, nchunk, cross, 0)

    @pl.when(s >= nseg)
    def _():
        base = (2 * nseg - 1 - s) * seg_len
        q_dec = jnp.exp(ld_b * (cs - ii))
        g_chunk = jnp.exp(chunk_len * ld_b)

        def products(c, carry):
            r0 = pl.multiple_of(c * cs, cs)
            rows = pl.ds(pl.multiple_of(base + r0, cs), cs)
            kv_scr[c] = lax.dot_general(kbwd[rows, :], v_ref[0, pl.ds(r0, cs), :], tn,
                                        preferred_element_type=F32)
            return carry

        lax.fori_loop(0, nchunk, products, 0)

        def scan(t, carry):
            c = nchunk - 1 - t
            r = rstate[1]
            rprev[c] = r.astype(BF16)
            rstate[1] = g_chunk * r + kv_scr[c]
            return carry

        lax.fori_loop(0, nchunk, scan, 0)

        def finish(c, carry):
            r0 = pl.multiple_of(c * cs, cs)
            rows = pl.ds(pl.multiple_of(base + r0, cs), cs)
            o = oacc[rows, :] + q_dec * jnp.dot(qrot[rows, :], rprev[c], preferred_element_type=F32)
            mu = jnp.mean(o, axis=-1, keepdims=True)
            oc = o - mu
            var = jnp.mean(oc * oc, axis=-1, keepdims=True)
            gate = g_ref[0, pl.ds(r0, cs), :].astype(F32)
            silu = gate * (1.0 / (1.0 + jnp.exp(-gate)))
            o_ref[0, pl.ds(r0, cs), :] = (oc * lax.rsqrt(var + EPS) * silu).astype(o_ref.dtype)
            return carry

        lax.fori_loop(0, nchunk, finish, 0)


def _rotary_tables(seq_len):
    half = RET_HEAD_DIM // 2
    inv = 1.0 / (10000.0 ** jnp.linspace(0.0, 1.0, half, dtype=F32))
    ang = jnp.arange(seq_len, dtype=F32)[:, None] * inv[None, :]
    return jnp.cos(ang), jnp.sin(ang)


def retention(proj, log_decay, cos, sin):
    b, s, _ = proj.shape
    seg_len = min(RET_SEG, s)
    nseg = s // seg_len
    nchunk = seg_len // RET_CHUNK
    base = (ATT_WIDTH + 2 * KV_WIDTH) // RET_HEAD_DIM

    def seg_both(t):
        return jnp.where(t < nseg, t, 2 * nseg - 1 - t)

    def seg_fwd_only(t):
        return jnp.minimum(t, nseg - 1)

    def seg_bwd_only(t):
        return jnp.where(t < nseg, nseg - 1, 2 * nseg - 1 - t)

    def head_block(which, seg):
        off = base + which * N_RET_HEADS
        return pl.BlockSpec((1, seg_len, RET_HEAD_DIM), lambda bi, h, t: (bi, seg(t), off + h))

    table = pl.BlockSpec((seg_len, RET_HEAD_DIM // 2), lambda bi, h, t: (seg_fwd_only(t), 0))
    return pl.pallas_call(
        functools.partial(_ret_kernel, nseg=nseg, seg_len=seg_len),
        grid=(b, N_RET_HEADS, 2 * nseg),
        in_specs=[pl.BlockSpec(memory_space=pltpu.SMEM),
                  head_block(0, seg_fwd_only), head_block(1, seg_fwd_only),
                  head_block(2, seg_both), head_block(3, seg_bwd_only), table, table],
        out_specs=pl.BlockSpec((1, seg_len, RET_HEAD_DIM),
                               lambda bi, h, t: (bi, seg_bwd_only(t), h)),
        out_shape=jax.ShapeDtypeStruct((b, s, RET_WIDTH), BF16),
        scratch_shapes=[pltpu.VMEM((s, RET_HEAD_DIM), F32),
                        pltpu.VMEM((s, RET_HEAD_DIM), BF16),
                        pltpu.VMEM((s, RET_HEAD_DIM), BF16),
                        pltpu.VMEM((nchunk, RET_HEAD_DIM, RET_HEAD_DIM), F32),
                        pltpu.VMEM((nchunk, RET_HEAD_DIM, RET_HEAD_DIM), BF16),
                        pltpu.VMEM((2, RET_HEAD_DIM, RET_HEAD_DIM), F32)],
        compiler_params=_params("parallel", "arbitrary", "arbitrary"),
        name="retention",
    )(log_decay, proj, proj, proj, proj, cos, sin)


def _outproj_kernel(att_ref, ret_ref, x_ref, wa_ref, wr_ref, g_ref, wrt_ref, xo_ref, hn_ref, aff_ref):
    y = x_ref[...] + jnp.dot(att_ref[...], wa_ref[...], preferred_element_type=F32)
    y = y + jnp.dot(ret_ref[...], wr_ref[...], preferred_element_type=F32)
    xo_ref[...] = y
    var = jnp.mean(y * y, axis=-1, keepdims=True)
    hn = (y * lax.rsqrt(var + EPS) * g_ref[...]).astype(BF16)
    hn_ref[...] = hn
    logits = jnp.dot(hn, wrt_ref[...], preferred_element_type=F32)
    lane = lax.broadcasted_iota(jnp.int32, (1, LANES), 1)
    logits = jnp.where(lane < N_EXPERTS, logits, NEG_INF)
    e = jnp.exp(logits - jnp.max(logits, axis=-1, keepdims=True))
    aff_ref[...] = e / jnp.sum(e, axis=-1, keepdims=True)


def outproj_norm_router(att, ret, x2d, w_out_bf16, g, w_router_pad):
    n, d = x2d.shape
    tm = min(ROW_TILE_OUTPROJ, n)
    row = lambda i: (i, 0)
    return pl.pallas_call(
        _outproj_kernel,
        grid=(n // tm,),
        in_specs=[
            pl.BlockSpec((tm, ATT_WIDTH), row),
            pl.BlockSpec((tm, RET_WIDTH), row),
            pl.BlockSpec((tm, d), row),
            pl.BlockSpec((ATT_WIDTH, d), lambda i: (0, 0)),
            pl.BlockSpec((RET_WIDTH, d), lambda i: (1, 0)),
            pl.BlockSpec((1, d), lambda i: (0, 0)),
            pl.BlockSpec((d, LANES), lambda i: (0, 0)),
        ],
        out_specs=[pl.BlockSpec((tm, d), row), pl.BlockSpec((tm, d), row),
                   pl.BlockSpec((tm, LANES), row)],
        out_shape=[jax.ShapeDtypeStruct((n, d), F32), jax.ShapeDtypeStruct((n, d), BF16),
                   jax.ShapeDtypeStruct((n, LANES), F32)],
        compiler_params=_params("parallel"),
        name="outproj_norm_router",
    )(att, ret, x2d, w_out_bf16, w_out_bf16, g.reshape(1, d), w_router_pad)


def _moe_kernel(xe_ref, wg_ref, wu_ref, wd_ref, gate_ref, o_ref):
    x = xe_ref[0]
    hg = jnp.dot(x, wg_ref[0], preferred_element_type=F32)
    hu = jnp.dot(x, wu_ref[0], preferred_element_type=F32)
    h = hg * (1.0 / (1.0 + jnp.exp(-hg))) * hu
    y = jnp.dot(h.astype(BF16), wd_ref[0], preferred_element_type=F32)
    o_ref[0] = y * gate_ref[0]


def expert_ffn(xe, gates, wg, wu, wd):
    e, cap, d = xe.shape
    ff = wg.shape[-1]
    tc = min(MOE_SLOT_TILE, cap)
    return pl.pallas_call(
        _moe_kernel,
        grid=(e, cap // tc),
        in_specs=[
            pl.BlockSpec((1, tc, d), lambda ei, t: (ei, t, 0)),
            pl.BlockSpec((1, d, ff), lambda ei, t: (ei, 0, 0)),
            pl.BlockSpec((1, d, ff), lambda ei, t: (ei, 0, 0)),
            pl.BlockSpec((1, ff, d), lambda ei, t: (ei, 0, 0)),
            pl.BlockSpec((1, tc, 1), lambda ei, t: (ei, t, 0)),
        ],
        out_specs=pl.BlockSpec((1, tc, d), lambda ei, t: (ei, t, 0)),
        out_shape=jax.ShapeDtypeStruct((e, cap, d), F32),
        compiler_params=_params("parallel", "arbitrary"),
        name="expert_ffn",
    )(xe, wg, wu, wd, gates)


def _norm_kernel(x_ref, g_ref, o_ref):
    x = x_ref[...]
    var = jnp.mean(x * x, axis=-1, keepdims=True)
    o_ref[...] = x * lax.rsqrt(var + EPS) * g_ref[...]


def final_norm(x2d, g):
    n, d = x2d.shape
    tm = min(NORM_ROW_TILE, n)
    return pl.pallas_call(
        _norm_kernel,
        grid=(n // tm,),
        in_specs=[pl.BlockSpec((tm, d), lambda i: (i, 0)), pl.BlockSpec((1, d), lambda i: (0, 0))],
        out_specs=pl.BlockSpec((tm, d), lambda i: (i, 0)),
        out_shape=jax.ShapeDtypeStruct((n, d), F32),
        compiler_params=_params("parallel"),
        name="final_norm",
    )(x2d, g.reshape(1, d))


def _layer(x2d, bsz, seq, lw, band_bias, cos, sin):
    n = bsz * seq
    proj = norm_inproj(x2d, lw["norm_mix_g"], lw["w_in"]).reshape(bsz, seq, IN_WIDTH)
    att = window_attention(proj, band_bias, lw["attn_sink"]).reshape(n, ATT_WIDTH)
    ret = retention(proj, lw["log_decay"], cos, sin).reshape(n, RET_WIDTH)
    x2d, hn, aff = outproj_norm_router(att, ret, x2d, lw["w_out"], lw["norm_ffn_g"], lw["w_router"])
    cap = CAPACITY_FACTOR * n // N_EXPERTS
    gates, idx = lax.top_k(aff[:, :N_EXPERTS].T, cap)
    ye = expert_ffn(hn[idx], gates[..., None], lw["w_gate"], lw["w_up"], lw["w_down"])
    return x2d.at[idx.reshape(-1)].add(ye.reshape(-1, D_MODEL))


def kernel(x_prompt, x_sample, norm_mix_g, w_in, attn_sink, rel_bias, ret_decay_fwd, ret_decay_bwd,
           w_out, norm_ffn_g, w_router, w_gate, w_up, w_down, norm_final_g):
    depth = w_in.shape[0]
    band_bias = _band_bias(rel_bias)
    layers = []
    for l in range(depth):
        layers.append({
            "norm_mix_g": norm_mix_g[l],
            "w_in": w_in[l].astype(BF16),
            "attn_sink": attn_sink[l],
            "log_decay": jnp.stack([jax.nn.log_sigmoid(ret_decay_fwd[l].astype(F32)),
                                    jax.nn.log_sigmoid(ret_decay_bwd[l].astype(F32))]),
            "w_out": w_out[l].astype(BF16),
            "norm_ffn_g": norm_ffn_g[l],
            "w_router": jnp.pad(w_router[l].astype(BF16), ((0, 0), (0, LANES - N_EXPERTS))),
            "w_gate": w_gate[l].astype(BF16),
            "w_up": w_up[l].astype(BF16),
            "w_down": w_down[l].astype(BF16),
        })

    def trunk(x):
        bsz, seq, d = x.shape
        cos, sin = _rotary_tables(seq)
        x2d = x.reshape(bsz * seq, d)
        for lw in layers:
            x2d = _layer(x2d, bsz, seq, lw, band_bias, cos, sin)
        return final_norm(x2d, norm_final_g).reshape(bsz, seq, d)

    return (trunk(x_prompt), trunk(x_sample))
```

```python
import functools
import math

import jax
import jax.numpy as jnp
import numpy as np
from jax import lax
from jax.experimental import pallas as pl
from jax.experimental.pallas import tpu as pltpu

D_MODEL = 2048
HEAD_DIM_ATT = 128
N_Q_HEADS = 8
N_KV_HEADS = 2
GQA = N_Q_HEADS // N_KV_HEADS
ATT_WIDTH = N_Q_HEADS * HEAD_DIM_ATT
KV_WIDTH = N_KV_HEADS * HEAD_DIM_ATT
WINDOW = 128
BLOCK = 128
N_BUCKETS = 32
MAX_DISTANCE = 128
N_RET_HEADS = 4
RET_HEAD_DIM = 256
RET_WIDTH = N_RET_HEADS * RET_HEAD_DIM
RET_CHUNK = 256
D_MIX = ATT_WIDTH + RET_WIDTH
IN_WIDTH = ATT_WIDTH + 2 * KV_WIDTH + 4 * RET_WIDTH
N_EXPERTS = 16
CAPACITY_FACTOR = 2
EXPERT_FF = D_MODEL // 2
EPS = 1e-6
NEG_INF = -1e30
LANES = 128

VMEM_LIMIT_BYTES = 56 * 1024 * 1024

ROW_TILE_INPROJ = 1024
COL_TILE_INPROJ = 512
ROW_TILE_OUTPROJ = 512
ATT_Q_TILE = 512
RET_SEG = 2048
MOE_SLOT_TILE = 512
COMBINE_TOKEN_TILE = 512
COMBINE_ROW_BLOCK = 256

BF16 = jnp.bfloat16
F32 = jnp.float32


def _params(*sem):
    return pltpu.CompilerParams(dimension_semantics=sem, vmem_limit_bytes=VMEM_LIMIT_BYTES)


def _norm_inproj_kernel(x_ref, g_ref, w_ref, o_ref, h_ref):
    @pl.when(pl.program_id(1) == 0)
    def _():
        x = x_ref[...]
        var = jnp.mean(x * x, axis=-1, keepdims=True)
        h_ref[...] = (x * lax.rsqrt(var + EPS) * g_ref[...]).astype(BF16)

    o_ref[...] = jnp.dot(h_ref[...], w_ref[...], preferred_element_type=F32).astype(o_ref.dtype)


def norm_inproj(x2d, g, w_bf16):
    n, d = x2d.shape
    width = w_bf16.shape[1]
    tm = min(ROW_TILE_INPROJ, n)
    tn = COL_TILE_INPROJ
    return pl.pallas_call(
        _norm_inproj_kernel,
        grid=(n // tm, width // tn),
        in_specs=[
            pl.BlockSpec((tm, d), lambda i, j: (i, 0)),
            pl.BlockSpec((1, d), lambda i, j: (0, 0)),
            pl.BlockSpec((d, tn), lambda i, j: (0, j)),
        ],
        out_specs=pl.BlockSpec((tm, tn), lambda i, j: (i, j)),
        out_shape=jax.ShapeDtypeStruct((n, width), BF16),
        scratch_shapes=[pltpu.VMEM((tm, d), BF16)],
        compiler_params=_params("parallel", "arbitrary"),
        name="norm_inproj",
    )(x2d, g.reshape(1, d), w_bf16)


def _t5_bucket(rel):
    half = N_BUCKETS // 2
    max_exact = half // 2
    bucket = jnp.where(rel > 0, half, 0)
    n = jnp.abs(rel)
    nf = jnp.maximum(n, 1).astype(F32)
    large = max_exact + (jnp.log(nf / max_exact) / math.log(MAX_DISTANCE / max_exact)
                         * (half - max_exact)).astype(jnp.int32)
    large = jnp.minimum(large, half - 1)
    return bucket + jnp.where(n < max_exact, n, large)


def _band_bias(rel_bias):
    qi = jnp.arange(BLOCK)[:, None]
    kj = jnp.arange(3 * BLOCK)[None, :]
    rel = kj - BLOCK - qi
    bias = rel_bias.astype(F32)[_t5_bucket(rel)]
    bias = jnp.where((jnp.abs(rel) <= WINDOW)[:, :, None], bias, NEG_INF)
    bias = bias.reshape(BLOCK, 3, BLOCK, N_KV_HEADS, GQA).transpose(3, 1, 2, 4, 0)
    bias = bias.reshape(N_KV_HEADS, 3, BLOCK, GQA * BLOCK)
    return jnp.concatenate([bias, jnp.full_like(bias[:, :1], NEG_INF)], axis=1)


def _attn_kernel(q_ref, kp_ref, km_ref, kn_ref, vp_ref, vm_ref, vn_ref, bias_ref, sink_ref,
                 o_ref, k_scr, v_scr, *, tq):
    i = pl.program_id(1)
    last = pl.num_programs(1) - 1
    k_scr[0:BLOCK] = kp_ref[0]
    k_scr[BLOCK:BLOCK + tq] = km_ref[0]
    k_scr[BLOCK + tq:] = kn_ref[0]
    v_scr[0:BLOCK] = vp_ref[0]
    v_scr[BLOCK:BLOCK + tq] = vm_ref[0]
    v_scr[BLOCK + tq:] = vn_ref[0]
    scale = HEAD_DIM_ATT ** -0.5
    nsb = tq // BLOCK
    nt = (((1,), (1,)), ((), ()))
    tn = (((0,), (0,)), ((), ()))
    for sb in range(nsb):
        for h in range(N_KV_HEADS):
            c0 = h * HEAD_DIM_ATT
            qs = jnp.concatenate(
                [q_ref[0, sb * BLOCK:(sb + 1) * BLOCK,
                       (h * GQA + g) * HEAD_DIM_ATT:(h * GQA + g + 1) * HEAD_DIM_ATT]
                 for g in range(GQA)], axis=0)
            parts = []
            for c in range(3):
                kc = k_scr[(sb + c) * BLOCK:(sb + c + 1) * BLOCK, c0:c0 + HEAD_DIM_ATT]
                if c == 0 and sb == 0:
                    slot = jnp.where(i == 0, 3, 0)
                elif c == 2 and sb == nsb - 1:
                    slot = jnp.where(i == last, 3, 2)
                else:
                    slot = c
                s_c = lax.dot_general(kc, qs, nt, preferred_element_type=F32)
                parts.append(s_c * scale + bias_ref[h, slot])
            s = jnp.concatenate(parts, axis=0)
            sink = sink_ref[h:h + 1, :]
            m = jnp.maximum(jnp.max(s, axis=0, keepdims=True), sink)
            p = jnp.exp(s - m)
            denom = jnp.sum(p, axis=0, keepdims=True) + jnp.exp(sink - m)
            pn = (p * (1.0 / denom)).astype(BF16)
            vb = v_scr[sb * BLOCK:(sb + 3) * BLOCK, c0:c0 + HEAD_DIM_ATT]
            o = lax.dot_general(pn, vb, tn, preferred_element_type=F32)
            for g in range(GQA):
                o_ref[0, sb * BLOCK:(sb + 1) * BLOCK,
                      (h * GQA + g) * HEAD_DIM_ATT:(h * GQA + g + 1) * HEAD_DIM_ATT] = (
                    o[g * BLOCK:(g + 1) * BLOCK].astype(o_ref.dtype))


def window_attention(proj, band_bias, sink):
    b, s, _ = proj.shape
    tq = min(ATT_Q_TILE, s)
    nblk = s // BLOCK
    r = tq // BLOCK
    assert r >= 2 and s % tq == 0
    k_col = ATT_WIDTH // KV_WIDTH
    v_col = k_col + 1

    def main(col):
        return pl.BlockSpec((1, tq, KV_WIDTH), lambda bi, i: (bi, i, col))

    def prev(col):
        return pl.BlockSpec((1, BLOCK, KV_WIDTH), lambda bi, i: (bi, jnp.maximum(i * r - 1, 0), col))

    def nxt(col):
        return pl.BlockSpec((1, BLOCK, KV_WIDTH),
                            lambda bi, i: (bi, jnp.minimum((i + 1) * r, nblk - 1), col))

    sink_rows = jnp.repeat(sink.astype(F32), BLOCK).reshape(N_KV_HEADS, GQA * BLOCK)
    return pl.pallas_call(
        functools.partial(_attn_kernel, tq=tq),
        grid=(b, s // tq),
        in_specs=[
            pl.BlockSpec((1, tq, ATT_WIDTH), lambda bi, i: (bi, i, 0)),
            prev(k_col), main(k_col), nxt(k_col),
            prev(v_col), main(v_col), nxt(v_col),
            pl.BlockSpec((N_KV_HEADS, 4, BLOCK, GQA * BLOCK), lambda bi, i: (0, 0, 0, 0)),
            pl.BlockSpec((N_KV_HEADS, GQA * BLOCK), lambda bi, i: (0, 0)),
        ],
        out_specs=pl.BlockSpec((1, tq, ATT_WIDTH), lambda bi, i: (bi, i, 0)),
        out_shape=jax.ShapeDtypeStruct((b, s, ATT_WIDTH), BF16),
        scratch_shapes=[pltpu.VMEM((tq + 2 * BLOCK, KV_WIDTH), BF16),
                        pltpu.VMEM((tq + 2 * BLOCK, KV_WIDTH), BF16)],
        compiler_params=_params("parallel", "arbitrary"),
        name="window_attention",
    )(proj, proj, proj, proj, proj, proj, proj, band_bias, sink_rows)


def _rotate(x, cos, sin):
    half = RET_HEAD_DIM // 2
    x1 = x[:, :half]
    x2 = x[:, half:]
    return jnp.concatenate([x1 * cos - x2 * sin, x1 * sin + x2 * cos], axis=-1)


def _ret_kernel(ld_ref, q_ref, k_ref, v_ref, g_ref, cos_ref, sin_ref, o_ref,
                oacc, qrot, kbwd, kv_scr, rprev, rstate, *, nseg, seg_len):
    h = pl.program_id(1)
    s = pl.program_id(2)
    cs = RET_CHUNK
    nchunk = seg_len // cs
    ld_f = ld_ref[0, h]
    ld_b = ld_ref[1, h]
    ii = lax.broadcasted_iota(jnp.int32, (cs, 1), 0).astype(F32)
    jj = lax.broadcasted_iota(jnp.int32, (1, cs), 1).astype(F32)
    k_scale = RET_HEAD_DIM ** -0.5
    nt = (((1,), (1,)), ((), ()))
    tn = (((0,), (0,)), ((), ()))
    chunk_len = jnp.full((1, 1), cs, F32)

    @pl.when(s == 0)
    def _():
        rstate[...] = jnp.zeros_like(rstate)

    @pl.when(s < nseg)
    def _():
        base = s * seg_len
        diff = ii - jj
        dmat = k_scale * jnp.where(diff >= 0, jnp.exp(ld_f * jnp.maximum(diff, 0.0)),
                                   jnp.exp(ld_b * jnp.maximum(-diff, 0.0)))
        q_dec = jnp.exp(ld_f * (ii + 1.0))
        k_dec_f = k_scale * jnp.exp(ld_f * (cs - 1.0 - ii))
        k_dec_b = k_scale * jnp.exp(ld_b * ii)
        g_chunk = jnp.exp(chunk_len * ld_f)

        def products(c, carry):
            r0 = pl.multiple_of(c * cs, cs)
            rows = pl.ds(pl.multiple_of(base + r0, cs), cs)
            cos = cos_ref[pl.ds(r0, cs), :]
            sin = sin_ref[pl.ds(r0, cs), :]
            q = _rotate(q_ref[0, pl.ds(r0, cs), :].astype(F32), cos, sin).astype(BF16)
            k = _rotate(k_ref[0, pl.ds(r0, cs), :].astype(F32), cos, sin)
            v = v_ref[0, pl.ds(r0, cs), :]
            qrot[rows, :] = q
            kbwd[rows, :] = (k * k_dec_b).astype(BF16)
            a = lax.dot_general(q, k.astype(BF16), nt, preferred_element_type=F32) * dmat
            oacc[rows, :] = jnp.dot(a.astype(BF16), v, preferred_element_type=F32)
            kv_scr[c] = lax.dot_general((k * k_dec_f).astype(BF16), v, tn, preferred_element_type=F32)
            return carry

        for c_static in range(nchunk):
            products(c_static, 0)

        def scan(c, carry):
            r = rstate[0]
            rprev[c] = r.astype(BF16)
            rstate[0] = g_chunk * r + kv_scr[c]
            return carry

        for c_static in range(nchunk):
            scan(c_static, 0)

        def cross(c, carry):
            rows = pl.ds(pl.multiple_of(base + c * cs, cs), cs)
            oacc[rows, :] += q_dec * jnp.dot(qrot[rows, :], rprev[c], preferred_element_type=F32)
            return carry

        for c_static in range(nchunk):
            cross(c_static, 0)

    @pl.when(s >= nseg)
    def _():
        base = (2 * nseg - 1 - s) * seg_len
        q_dec = jnp.exp(ld_b * (cs - ii))
        g_chunk = jnp.exp(chunk_len * ld_b)

        def products(c, carry):
            r0 = pl.multiple_of(c * cs, cs)
            rows = pl.ds(pl.multiple_of(base + r0, cs), cs)
            kv_scr[c] = lax.dot_general(kbwd[rows, :], v_ref[0, pl.ds(r0, cs), :], tn,
                                        preferred_element_type=F32)
            return carry

        for c_static in range(nchunk):
            products(c_static, 0)

        def scan(t, carry):
            c = nchunk - 1 - t
            r = rstate[1]
            rprev[c] = r.astype(BF16)
            rstate[1] = g_chunk * r + kv_scr[c]
            return carry

        for c_static in range(nchunk):
            scan(c_static, 0)

        def finish(c, carry):
            r0 = pl.multiple_of(c * cs, cs)
            rows = pl.ds(pl.multiple_of(base + r0, cs), cs)
            o = oacc[rows, :] + q_dec * jnp.dot(qrot[rows, :], rprev[c], preferred_element_type=F32)
            mu = jnp.mean(o, axis=-1, keepdims=True)
            oc = o - mu
            var = jnp.mean(oc * oc, axis=-1, keepdims=True)
            gate = g_ref[0, pl.ds(r0, cs), :].astype(F32)
            silu = gate * (1.0 / (1.0 + jnp.exp(-gate)))
            o_ref[0, pl.ds(r0, cs), :] = (oc * lax.rsqrt(var + EPS) * silu).astype(o_ref.dtype)
            return carry

        for c_static in range(nchunk):
            finish(c_static, 0)


def _rotary_tables(seq_len):
    half = RET_HEAD_DIM // 2
    inv = 1.0 / (10000.0 ** jnp.linspace(0.0, 1.0, half, dtype=F32))
    ang = jnp.arange(seq_len, dtype=F32)[:, None] * inv[None, :]
    return jnp.cos(ang), jnp.sin(ang)


def retention(proj, log_decay, cos, sin):
    b, s, _ = proj.shape
    seg_len = min(RET_SEG, s)
    nseg = s // seg_len
    nchunk = seg_len // RET_CHUNK
    base = (ATT_WIDTH + 2 * KV_WIDTH) // RET_HEAD_DIM

    def seg_both(t):
        return jnp.where(t < nseg, t, 2 * nseg - 1 - t)

    def seg_fwd_only(t):
        return jnp.minimum(t, nseg - 1)

    def seg_bwd_only(t):
        return jnp.where(t < nseg, nseg - 1, 2 * nseg - 1 - t)

    def head_block(which, seg):
        off = base + which * N_RET_HEADS
        return pl.BlockSpec((1, seg_len, RET_HEAD_DIM), lambda bi, h, t: (bi, seg(t), off + h))

    table = pl.BlockSpec((seg_len, RET_HEAD_DIM // 2), lambda bi, h, t: (seg_fwd_only(t), 0))
    return pl.pallas_call(
        functools.partial(_ret_kernel, nseg=nseg, seg_len=seg_len),
        grid=(b, N_RET_HEADS, 2 * nseg),
        in_specs=[pl.BlockSpec(memory_space=pltpu.SMEM),
                  head_block(0, seg_fwd_only), head_block(1, seg_fwd_only),
                  head_block(2, seg_both), head_block(3, seg_bwd_only), table, table],
        out_specs=pl.BlockSpec((1, seg_len, RET_HEAD_DIM),
                               lambda bi, h, t: (bi, seg_bwd_only(t), h)),
        out_shape=jax.ShapeDtypeStruct((b, s, RET_WIDTH), BF16),
        scratch_shapes=[pltpu.VMEM((s, RET_HEAD_DIM), F32),
                        pltpu.VMEM((s, RET_HEAD_DIM), BF16),
                        pltpu.VMEM((s, RET_HEAD_DIM), BF16),
                        pltpu.VMEM((nchunk, RET_HEAD_DIM, RET_HEAD_DIM), F32),
                        pltpu.VMEM((nchunk, RET_HEAD_DIM, RET_HEAD_DIM), BF16),
                        pltpu.VMEM((2, RET_HEAD_DIM, RET_HEAD_DIM), F32)],
        compiler_params=_params("parallel", "arbitrary", "arbitrary"),
        name="retention",
    )(log_decay, proj, proj, proj, proj, cos, sin)


def _outproj_kernel(att_ref, ret_ref, x_ref, wa_ref, wr_ref, g_ref, wrt_ref, xo_ref, hn_ref, aff_ref):
    y = x_ref[...] + jnp.dot(att_ref[...], wa_ref[...], preferred_element_type=F32)
    y = y + jnp.dot(ret_ref[...], wr_ref[...], preferred_element_type=F32)
    xo_ref[...] = y
    var = jnp.mean(y * y, axis=-1, keepdims=True)
    hn = (y * lax.rsqrt(var + EPS) * g_ref[...]).astype(BF16)
    hn_ref[...] = hn
    logits = jnp.dot(hn, wrt_ref[...], preferred_element_type=F32)
    lane = lax.broadcasted_iota(jnp.int32, (1, LANES), 1)
    logits = jnp.where(lane < N_EXPERTS, logits, NEG_INF)
    e = jnp.exp(logits - jnp.max(logits, axis=-1, keepdims=True))
    aff_ref[...] = e / jnp.sum(e, axis=-1, keepdims=True)


def outproj_norm_router(att, ret, x2d, w_out_bf16, g, w_router_pad):
    n, d = x2d.shape
    tm = min(ROW_TILE_OUTPROJ, n)
    row = lambda i: (i, 0)
    return pl.pallas_call(
        _outproj_kernel,
        grid=(n // tm,),
        in_specs=[
            pl.BlockSpec((tm, ATT_WIDTH), row),
            pl.BlockSpec((tm, RET_WIDTH), row),
            pl.BlockSpec((tm, d), row),
            pl.BlockSpec((ATT_WIDTH, d), lambda i: (0, 0)),
            pl.BlockSpec((RET_WIDTH, d), lambda i: (1, 0)),
            pl.BlockSpec((1, d), lambda i: (0, 0)),
            pl.BlockSpec((d, LANES), lambda i: (0, 0)),
        ],
        out_specs=[pl.BlockSpec((tm, d), row), pl.BlockSpec((tm, d), row),
                   pl.BlockSpec((tm, LANES), row)],
        out_shape=[jax.ShapeDtypeStruct((n, d), F32), jax.ShapeDtypeStruct((n, d), BF16),
                   jax.ShapeDtypeStruct((n, LANES), F32)],
        compiler_params=_params("parallel"),
        name="outproj_norm_router",
    )(att, ret, x2d, w_out_bf16, w_out_bf16, g.reshape(1, d), w_router_pad)


def _moe_kernel(xe_ref, wg_ref, wu_ref, wd_ref, gate_ref, o_ref):
    x = xe_ref[0]
    hg = jnp.dot(x, wg_ref[0], preferred_element_type=F32)
    hu = jnp.dot(x, wu_ref[0], preferred_element_type=F32)
    h = hg * (1.0 / (1.0 + jnp.exp(-hg))) * hu
    y = jnp.dot(h.astype(BF16), wd_ref[0], preferred_element_type=F32)
    o_ref[0] = (y * gate_ref[0]).astype(o_ref.dtype)


def expert_ffn(xe, gates, wg, wu, wd):
    e, cap, d = xe.shape
    ff = wg.shape[-1]
    tc = min(MOE_SLOT_TILE, cap)
    return pl.pallas_call(
        _moe_kernel,
        grid=(e, cap // tc),
        in_specs=[
            pl.BlockSpec((1, tc, d), lambda ei, t: (ei, t, 0)),
            pl.BlockSpec((1, d, ff), lambda ei, t: (ei, 0, 0)),
            pl.BlockSpec((1, d, ff), lambda ei, t: (ei, 0, 0)),
            pl.BlockSpec((1, ff, d), lambda ei, t: (ei, 0, 0)),
            pl.BlockSpec((1, tc, 1), lambda ei, t: (ei, t, 0)),
        ],
        out_specs=pl.BlockSpec((1, tc, d), lambda ei, t: (ei, t, 0)),
        out_shape=jax.ShapeDtypeStruct((e, cap, d), BF16),
        compiler_params=_params("parallel", "arbitrary"),
        name="expert_ffn",
    )(xe, wg, wu, wd, gates)


def _combine_kernel(lo_ref, hi_ref, x_ref, tok_ref, g_ref, ye_hbm, o_ref, buf, sem, *, tile, blk, apply_norm):
    i = pl.program_id(0)
    lo = lo_ref[i]
    hi = hi_ref[i]
    t0 = i * tile

    def fetch(c, slot):
        return pltpu.make_async_copy(ye_hbm.at[pl.ds(pl.multiple_of(c * blk, blk), blk)],
                                     buf.at[slot], sem.at[slot])

    @pl.when(lo < hi)
    def _():
        fetch(lo, 0).start()

    o_ref[...] = x_ref[...]
    row = lax.broadcasted_iota(jnp.int32, (tile, blk), 0)

    def body(c, carry):
        slot = lax.rem(c - lo, 2)
        fetch(c, slot).wait()

        @pl.when(c + 1 < hi)
        def _():
            fetch(c + 1, 1 - slot).start()

        rel = tok_ref[pl.ds(c, 1), :] - t0
        sel = jnp.where(row == rel, 1.0, 0.0).astype(BF16)
        o_ref[...] += jnp.dot(sel, buf[slot], preferred_element_type=F32)
        return carry

    lax.fori_loop(lo, hi, body, 0)
    if apply_norm:
        y = o_ref[...]
        var = jnp.mean(y * y, axis=-1, keepdims=True)
        o_ref[...] = y * lax.rsqrt(var + EPS) * g_ref[...]


def combine(x2d, ye_sorted, tok_sorted, final_g):
    n, d = x2d.shape
    rows = ye_sorted.shape[0]
    tile = min(COMBINE_TOKEN_TILE, n)
    blk = min(COMBINE_ROW_BLOCK, rows)
    bounds = jnp.searchsorted(tok_sorted, jnp.arange(0, n + 1, tile, dtype=jnp.int32)).astype(jnp.int32)
    start, end = bounds[:-1], bounds[1:]
    lo = start // blk
    hi = jnp.where(end > start, (end + blk - 1) // blk, lo)
    apply_norm = final_g is not None
    g = (final_g if apply_norm else jnp.ones((d,), F32)).reshape(1, d)
    return pl.pallas_call(
        functools.partial(_combine_kernel, tile=tile, blk=blk, apply_norm=apply_norm),
        grid=(n // tile,),
        in_specs=[pl.BlockSpec(memory_space=pltpu.SMEM), pl.BlockSpec(memory_space=pltpu.SMEM),
                  pl.BlockSpec((tile, d), lambda i: (i, 0)),
                  pl.BlockSpec((rows // blk, blk), lambda i: (0, 0)),
                  pl.BlockSpec((1, d), lambda i: (0, 0)),
                  pl.BlockSpec(memory_space=pl.ANY)],
        out_specs=pl.BlockSpec((tile, d), lambda i: (i, 0)),
        out_shape=jax.ShapeDtypeStruct((n, d), F32),
        scratch_shapes=[pltpu.VMEM((2, blk, d), BF16), pltpu.SemaphoreType.DMA((2,))],
        compiler_params=_params("arbitrary"),
        name="combine",
    )(lo, hi, x2d, tok_sorted.reshape(rows // blk, blk), g, ye_sorted)


def _layer(x2d, bsz, seq, lw, band_bias, cos, sin, final_g):
    n = bsz * seq
    proj = norm_inproj(x2d, lw["norm_mix_g"], lw["w_in"]).reshape(bsz, seq, IN_WIDTH)
    att = window_attention(proj, band_bias, lw["attn_sink"]).reshape(n, ATT_WIDTH)
    ret = retention(proj, lw["log_decay"], cos, sin).reshape(n, RET_WIDTH)
    x2d, hn, aff = outproj_norm_router(att, ret, x2d, lw["w_out"], lw["norm_ffn_g"], lw["w_router"])
    cap = CAPACITY_FACTOR * n // N_EXPERTS
    gates, idx = lax.top_k(aff[:, :N_EXPERTS].T, cap)
    ye = expert_ffn(hn[idx], gates[..., None], lw["w_gate"], lw["w_up"], lw["w_down"])
    slots = N_EXPERTS * cap
    tok_sorted, perm = lax.sort_key_val(idx.reshape(slots).astype(jnp.int32),
                                        jnp.arange(slots, dtype=jnp.int32))
    return combine(x2d, ye.reshape(slots, D_MODEL)[perm], tok_sorted, final_g)


def kernel(x_prompt, x_sample, norm_mix_g, w_in, attn_sink, rel_bias, ret_decay_fwd, ret_decay_bwd,
           w_out, norm_ffn_g, w_router, w_gate, w_up, w_down, norm_final_g):
    depth = w_in.shape[0]
    band_bias = _band_bias(rel_bias)
    layers = []
    for l in range(depth):
        layers.append({
            "norm_mix_g": norm_mix_g[l],
            "w_in": w_in[l].astype(BF16),
            "attn_sink": attn_sink[l],
            "log_decay": jnp.stack([jax.nn.log_sigmoid(ret_decay_fwd[l].astype(F32)),
                                    jax.nn.log_sigmoid(ret_decay_bwd[l].astype(F32))]),
            "w_out": w_out[l].astype(BF16),
            "norm_ffn_g": norm_ffn_g[l],
            "w_router": jnp.pad(w_router[l].astype(BF16), ((0, 0), (0, LANES - N_EXPERTS))),
            "w_gate": w_gate[l].astype(BF16),
            "w_up": w_up[l].astype(BF16),
            "w_down": w_down[l].astype(BF16),
        })

    def trunk(x):
        bsz, seq, d = x.shape
        cos, sin = _rotary_tables(seq)
        x2d = x.reshape(bsz * seq, d)
        for l, lw in enumerate(layers):
            x2d = _layer(x2d, bsz, seq, lw, band_bias, cos, sin,
                         norm_final_g if l == depth - 1 else None)
        return x2d.reshape(bsz, seq, d)

    return (trunk(x_prompt), trunk(x_sample))
```

```python
import functools
import math

import jax
import jax.numpy as jnp
import numpy as np
from jax import lax
from jax.experimental import pallas as pl
from jax.experimental.pallas import tpu as pltpu

D_MODEL = 2048
HEAD_DIM_ATT = 128
N_Q_HEADS = 8
N_KV_HEADS = 2
GQA = N_Q_HEADS // N_KV_HEADS
ATT_WIDTH = N_Q_HEADS * HEAD_DIM_ATT
KV_WIDTH = N_KV_HEADS * HEAD_DIM_ATT
WINDOW = 128
BLOCK = 128
N_BUCKETS = 32
MAX_DISTANCE = 128
N_RET_HEADS = 4
RET_HEAD_DIM = 256
RET_WIDTH = N_RET_HEADS * RET_HEAD_DIM
RET_CHUNK = 256
D_MIX = ATT_WIDTH + RET_WIDTH
IN_WIDTH = ATT_WIDTH + 2 * KV_WIDTH + 4 * RET_WIDTH
N_EXPERTS = 16
CAPACITY_FACTOR = 2
EXPERT_FF = D_MODEL // 2
EPS = 1e-6
NEG_INF = -1e30
LANES = 128
LOG2_E = math.log2(math.e)

VMEM_LIMIT_BYTES = 56 * 1024 * 1024

ROW_TILE_INPROJ = 1024
COL_TILE_INPROJ = 1408
ROW_TILE_OUTPROJ = 512
ATT_Q_TILE = 512
RET_SEG = 2048
MOE_SLOT_TILE = 512
COMBINE_TOKEN_TILE = 512
COMBINE_ROW_BLOCK = 256
COMBINE_BUFFERS = 4

BF16 = jnp.bfloat16
F32 = jnp.float32


def _params(*sem):
    return pltpu.CompilerParams(dimension_semantics=sem, vmem_limit_bytes=VMEM_LIMIT_BYTES)


def _norm_inproj_kernel(x_ref, g_ref, w_ref, o_ref, h_ref):
    @pl.when(pl.program_id(1) == 0)
    def _():
        x = x_ref[...]
        var = jnp.mean(x * x, axis=-1, keepdims=True)
        h_ref[...] = (x * lax.rsqrt(var + EPS) * g_ref[...]).astype(BF16)

    o_ref[...] = jnp.dot(h_ref[...], w_ref[...], preferred_element_type=F32).astype(o_ref.dtype)


def norm_inproj(x2d, g, w_bf16):
    n, d = x2d.shape
    width = w_bf16.shape[1]
    tm = min(ROW_TILE_INPROJ, n)
    tn = COL_TILE_INPROJ
    return pl.pallas_call(
        _norm_inproj_kernel,
        grid=(n // tm, width // tn),
        in_specs=[
            pl.BlockSpec((tm, d), lambda i, j: (i, 0)),
            pl.BlockSpec((1, d), lambda i, j: (0, 0)),
            pl.BlockSpec((d, tn), lambda i, j: (0, j)),
        ],
        out_specs=pl.BlockSpec((tm, tn), lambda i, j: (i, j)),
        out_shape=jax.ShapeDtypeStruct((n, width), BF16),
        scratch_shapes=[pltpu.VMEM((tm, d), BF16)],
        compiler_params=_params("parallel", "arbitrary"),
        name="norm_inproj",
    )(x2d, g.reshape(1, d), w_bf16)


def _t5_bucket(rel):
    half = N_BUCKETS // 2
    max_exact = half // 2
    bucket = jnp.where(rel > 0, half, 0)
    n = jnp.abs(rel)
    nf = jnp.maximum(n, 1).astype(F32)
    large = max_exact + (jnp.log(nf / max_exact) / math.log(MAX_DISTANCE / max_exact)
                         * (half - max_exact)).astype(jnp.int32)
    large = jnp.minimum(large, half - 1)
    return bucket + jnp.where(n < max_exact, n, large)


def _band_bias(rel_bias):
    qi = jnp.arange(BLOCK)[:, None]
    kj = jnp.arange(3 * BLOCK)[None, :]
    rel = kj - BLOCK - qi
    bias = rel_bias.astype(F32)[_t5_bucket(rel)]
    bias = jnp.where((jnp.abs(rel) <= WINDOW)[:, :, None], bias, NEG_INF)
    bias = bias.reshape(BLOCK, 3, BLOCK, N_KV_HEADS, GQA).transpose(3, 1, 2, 4, 0)
    bias = bias.reshape(N_KV_HEADS, 3, BLOCK, GQA * BLOCK)
    return jnp.concatenate([bias, jnp.full_like(bias[:, :1], NEG_INF)], axis=1) * LOG2_E


def _attn_kernel(q_ref, kp_ref, km_ref, kn_ref, vp_ref, vm_ref, vn_ref, bias_ref, sink_ref,
                 o_ref, k_scr, v_scr, *, tq):
    i = pl.program_id(1)
    last = pl.num_programs(1) - 1
    k_scr[0:BLOCK] = kp_ref[0]
    k_scr[BLOCK:BLOCK + tq] = km_ref[0]
    k_scr[BLOCK + tq:] = kn_ref[0]
    v_scr[0:BLOCK] = vp_ref[0]
    v_scr[BLOCK:BLOCK + tq] = vm_ref[0]
    v_scr[BLOCK + tq:] = vn_ref[0]
    scale = HEAD_DIM_ATT ** -0.5 * LOG2_E
    nsb = tq // BLOCK
    nt = (((1,), (1,)), ((), ()))
    tn = (((0,), (0,)), ((), ()))
    for sb in range(nsb):
        for h in range(N_KV_HEADS):
            c0 = h * HEAD_DIM_ATT
            qs = jnp.concatenate(
                [q_ref[0, sb * BLOCK:(sb + 1) * BLOCK,
                       (h * GQA + g) * HEAD_DIM_ATT:(h * GQA + g + 1) * HEAD_DIM_ATT]
                 for g in range(GQA)], axis=0)
            parts = []
            for c in range(3):
                kc = k_scr[(sb + c) * BLOCK:(sb + c + 1) * BLOCK, c0:c0 + HEAD_DIM_ATT]
                if c == 0 and sb == 0:
                    slot = jnp.where(i == 0, 3, 0)
                elif c == 2 and sb == nsb - 1:
                    slot = jnp.where(i == last, 3, 2)
                else:
                    slot = c
                s_c = lax.dot_general(kc, qs, nt, preferred_element_type=F32)
                parts.append(s_c * scale + bias_ref[h, slot])
            s = jnp.concatenate(parts, axis=0)
            sink = sink_ref[h:h + 1, :]
            m = jnp.maximum(jnp.max(s, axis=0, keepdims=True), sink)
            p = jnp.exp2(s - m)
            denom = jnp.sum(p, axis=0, keepdims=True) + jnp.exp2(sink - m)
            pn = (p * (1.0 / denom)).astype(BF16)
            vb = v_scr[sb * BLOCK:(sb + 3) * BLOCK, c0:c0 + HEAD_DIM_ATT]
            o = lax.dot_general(pn, vb, tn, preferred_element_type=F32)
            for g in range(GQA):
                o_ref[0, sb * BLOCK:(sb + 1) * BLOCK,
                      (h * GQA + g) * HEAD_DIM_ATT:(h * GQA + g + 1) * HEAD_DIM_ATT] = (
                    o[g * BLOCK:(g + 1) * BLOCK].astype(o_ref.dtype))


def window_attention(proj, band_bias, sink):
    b, s, _ = proj.shape
    tq = min(ATT_Q_TILE, s)
    nblk = s // BLOCK
    r = tq // BLOCK
    assert r >= 2 and s % tq == 0
    k_col = ATT_WIDTH // KV_WIDTH
    v_col = k_col + 1

    def main(col):
        return pl.BlockSpec((1, tq, KV_WIDTH), lambda bi, i: (bi, i, col))

    def prev(col):
        return pl.BlockSpec((1, BLOCK, KV_WIDTH), lambda bi, i: (bi, jnp.maximum(i * r - 1, 0), col))

    def nxt(col):
        return pl.BlockSpec((1, BLOCK, KV_WIDTH),
                            lambda bi, i: (bi, jnp.minimum((i + 1) * r, nblk - 1), col))

    sink_rows = jnp.repeat(sink.astype(F32) * LOG2_E, BLOCK).reshape(N_KV_HEADS, GQA * BLOCK)
    return pl.pallas_call(
        functools.partial(_attn_kernel, tq=tq),
        grid=(b, s // tq),
        in_specs=[
            pl.BlockSpec((1, tq, ATT_WIDTH), lambda bi, i: (bi, i, 0)),
            prev(k_col), main(k_col), nxt(k_col),
            prev(v_col), main(v_col), nxt(v_col),
            pl.BlockSpec((N_KV_HEADS, 4, BLOCK, GQA * BLOCK), lambda bi, i: (0, 0, 0, 0)),
            pl.BlockSpec((N_KV_HEADS, GQA * BLOCK), lambda bi, i: (0, 0)),
        ],
        out_specs=pl.BlockSpec((1, tq, ATT_WIDTH), lambda bi, i: (bi, i, 0)),
        out_shape=jax.ShapeDtypeStruct((b, s, ATT_WIDTH), BF16),
        scratch_shapes=[pltpu.VMEM((tq + 2 * BLOCK, KV_WIDTH), BF16),
                        pltpu.VMEM((tq + 2 * BLOCK, KV_WIDTH), BF16)],
        compiler_params=_params("parallel", "arbitrary"),
        name="window_attention",
    )(proj, proj, proj, proj, proj, proj, proj, band_bias, sink_rows)


def _rotate(x, cos, sin):
    half = RET_HEAD_DIM // 2
    x1 = x[:, :half]
    x2 = x[:, half:]
    return jnp.concatenate([x1 * cos - x2 * sin, x1 * sin + x2 * cos], axis=-1)


def _ret_kernel(ld_ref, q_ref, k_ref, v_ref, g_ref, cos_ref, sin_ref, o_ref,
                oacc, qrot, kbwd, kv_scr, rprev, rstate, *, nseg, seg_len):
    h = pl.program_id(1)
    s = pl.program_id(2)
    cs = RET_CHUNK
    nchunk = seg_len // cs
    ld_f = ld_ref[0, h]
    ld_b = ld_ref[1, h]
    ii = lax.broadcasted_iota(jnp.int32, (cs, 1), 0).astype(F32)
    jj = lax.broadcasted_iota(jnp.int32, (1, cs), 1).astype(F32)
    k_scale = RET_HEAD_DIM ** -0.5
    nt = (((1,), (1,)), ((), ()))
    tn = (((0,), (0,)), ((), ()))
    chunk_len = jnp.full((1, 1), cs, F32)

    @pl.when(s == 0)
    def _():
        rstate[...] = jnp.zeros_like(rstate)

    @pl.when(s < nseg)
    def _():
        base = s * seg_len
        diff = ii - jj
        dmat = k_scale * jnp.where(diff >= 0, jnp.exp(ld_f * jnp.maximum(diff, 0.0)),
                                   jnp.exp(ld_b * jnp.maximum(-diff, 0.0)))
        q_dec = jnp.exp(ld_f * (ii + 1.0))
        k_dec_f = k_scale * jnp.exp(ld_f * (cs - 1.0 - ii))
        k_dec_b = k_scale * jnp.exp(ld_b * ii)
        g_chunk = jnp.exp(chunk_len * ld_f)

        def products(c, carry):
            r0 = pl.multiple_of(c * cs, cs)
            rows = pl.ds(pl.multiple_of(base + r0, cs), cs)
            cos = cos_ref[pl.ds(r0, cs), :]
            sin = sin_ref[pl.ds(r0, cs), :]
            q = _rotate(q_ref[0, pl.ds(r0, cs), :].astype(F32), cos, sin).astype(BF16)
            k = _rotate(k_ref[0, pl.ds(r0, cs), :].astype(F32), cos, sin)
            v = v_ref[0, pl.ds(r0, cs), :]
            qrot[rows, :] = q
            kbwd[rows, :] = (k * k_dec_b).astype(BF16)
            a = lax.dot_general(q, k.astype(BF16), nt, preferred_element_type=F32) * dmat
            oacc[rows, :] = jnp.dot(a.astype(BF16), v, preferred_element_type=F32)
            kv_scr[c] = lax.dot_general((k * k_dec_f).astype(BF16), v, tn, preferred_element_type=F32)
            return carry

        for c_static in range(nchunk):
            products(c_static, 0)

        def scan(c, carry):
            r = rstate[0]
            rprev[c] = r.astype(BF16)
            rstate[0] = g_chunk * r + kv_scr[c]
            return carry

        for c_static in range(nchunk):
            scan(c_static, 0)

        def cross(c, carry):
            rows = pl.ds(pl.multiple_of(base + c * cs, cs), cs)
            oacc[rows, :] += q_dec * jnp.dot(qrot[rows, :], rprev[c], preferred_element_type=F32)
            return carry

        for c_static in range(nchunk):
            cross(c_static, 0)

    @pl.when(s >= nseg)
    def _():
        base = (2 * nseg - 1 - s) * seg_len
        q_dec = jnp.exp(ld_b * (cs - ii))
        g_chunk = jnp.exp(chunk_len * ld_b)

        def products(c, carry):
            r0 = pl.multiple_of(c * cs, cs)
            rows = pl.ds(pl.multiple_of(base + r0, cs), cs)
            kv_scr[c] = lax.dot_general(kbwd[rows, :], v_ref[0, pl.ds(r0, cs), :], tn,
                                        preferred_element_type=F32)
            return carry

        for c_static in range(nchunk):
            products(c_static, 0)

        def scan(t, carry):
            c = nchunk - 1 - t
            r = rstate[1]
            rprev[c] = r.astype(BF16)
            rstate[1] = g_chunk * r + kv_scr[c]
            return carry

        for c_static in range(nchunk):
            scan(c_static, 0)

        def finish(c, carry):
            r0 = pl.multiple_of(c * cs, cs)
            rows = pl.ds(pl.multiple_of(base + r0, cs), cs)
            o = oacc[rows, :] + q_dec * jnp.dot(qrot[rows, :], rprev[c], preferred_element_type=F32)
            mu = jnp.mean(o, axis=-1, keepdims=True)
            oc = o - mu
            var = jnp.mean(oc * oc, axis=-1, keepdims=True)
            gate = g_ref[0, pl.ds(r0, cs), :].astype(F32)
            silu = gate * (1.0 / (1.0 + jnp.exp(-gate)))
            o_ref[0, pl.ds(r0, cs), :] = (oc * lax.rsqrt(var + EPS) * silu).astype(o_ref.dtype)
            return carry

        for c_static in range(nchunk):
            finish(c_static, 0)


def _rotary_tables(seq_len):
    half = RET_HEAD_DIM // 2
    inv = 1.0 / (10000.0 ** jnp.linspace(0.0, 1.0, half, dtype=F32))
    ang = jnp.arange(seq_len, dtype=F32)[:, None] * inv[None, :]
    return jnp.cos(ang), jnp.sin(ang)


def retention(proj, log_decay, cos, sin):
    b, s, _ = proj.shape
    seg_len = min(RET_SEG, s)
    nseg = s // seg_len
    nchunk = seg_len // RET_CHUNK
    base = (ATT_WIDTH + 2 * KV_WIDTH) // RET_HEAD_DIM

    def seg_both(t):
        return jnp.where(t < nseg, t, 2 * nseg - 1 - t)

    def seg_fwd_only(t):
        return jnp.minimum(t, nseg - 1)

    def seg_bwd_only(t):
        return jnp.where(t < nseg, nseg - 1, 2 * nseg - 1 - t)

    def head_block(which, seg):
        off = base + which * N_RET_HEADS
        return pl.BlockSpec((1, seg_len, RET_HEAD_DIM), lambda bi, h, t: (bi, seg(t), off + h))

    table = pl.BlockSpec((seg_len, RET_HEAD_DIM // 2), lambda bi, h, t: (seg_fwd_only(t), 0))
    return pl.pallas_call(
        functools.partial(_ret_kernel, nseg=nseg, seg_len=seg_len),
        grid=(b, N_RET_HEADS, 2 * nseg),
        in_specs=[pl.BlockSpec(memory_space=pltpu.SMEM),
                  head_block(0, seg_fwd_only), head_block(1, seg_fwd_only),
                  head_block(2, seg_both), head_block(3, seg_bwd_only), table, table],
        out_specs=pl.BlockSpec((1, seg_len, RET_HEAD_DIM),
                               lambda bi, h, t: (bi, seg_bwd_only(t), h)),
        out_shape=jax.ShapeDtypeStruct((b, s, RET_WIDTH), BF16),
        scratch_shapes=[pltpu.VMEM((s, RET_HEAD_DIM), F32),
                        pltpu.VMEM((s, RET_HEAD_DIM), BF16),
                        pltpu.VMEM((s, RET_HEAD_DIM), BF16),
                        pltpu.VMEM((nchunk, RET_HEAD_DIM, RET_HEAD_DIM), F32),
                        pltpu.VMEM((nchunk, RET_HEAD_DIM, RET_HEAD_DIM), BF16),
                        pltpu.VMEM((2, RET_HEAD_DIM, RET_HEAD_DIM), F32)],
        compiler_params=_params("parallel", "arbitrary", "arbitrary"),
        name="retention",
    )(log_decay, proj, proj, proj, proj, cos, sin)


def _outproj_kernel(att_ref, ret_ref, x_ref, wa_ref, wr_ref, g_ref, wrt_ref, xo_ref, hn_ref, aff_ref):
    y = x_ref[...] + jnp.dot(att_ref[...], wa_ref[...], preferred_element_type=F32)
    y = y + jnp.dot(ret_ref[...], wr_ref[...], preferred_element_type=F32)
    xo_ref[...] = y
    var = jnp.mean(y * y, axis=-1, keepdims=True)
    hn = (y * lax.rsqrt(var + EPS) * g_ref[...]).astype(BF16)
    hn_ref[...] = hn
    logits = jnp.dot(hn, wrt_ref[...], preferred_element_type=F32)
    lane = lax.broadcasted_iota(jnp.int32, (1, LANES), 1)
    logits = jnp.where(lane < N_EXPERTS, logits, NEG_INF)
    e = jnp.exp(logits - jnp.max(logits, axis=-1, keepdims=True))
    aff = e / jnp.sum(e, axis=-1, keepdims=True)
    aff_ref[...] = jnp.transpose(aff)[:N_EXPERTS]


def outproj_norm_router(att, ret, x2d, w_out_bf16, g, w_router_pad):
    n, d = x2d.shape
    tm = min(ROW_TILE_OUTPROJ, n)
    row = lambda i: (i, 0)
    return pl.pallas_call(
        _outproj_kernel,
        grid=(n // tm,),
        in_specs=[
            pl.BlockSpec((tm, ATT_WIDTH), row),
            pl.BlockSpec((tm, RET_WIDTH), row),
            pl.BlockSpec((tm, d), row),
            pl.BlockSpec((ATT_WIDTH, d), lambda i: (0, 0)),
            pl.BlockSpec((RET_WIDTH, d), lambda i: (1, 0)),
            pl.BlockSpec((1, d), lambda i: (0, 0)),
            pl.BlockSpec((d, LANES), lambda i: (0, 0)),
        ],
        out_specs=[pl.BlockSpec((tm, d), row), pl.BlockSpec((tm, d), row),
                   pl.BlockSpec((N_EXPERTS, tm), lambda i: (0, i))],
        out_shape=[jax.ShapeDtypeStruct((n, d), F32), jax.ShapeDtypeStruct((n, d), BF16),
                   jax.ShapeDtypeStruct((N_EXPERTS, n), F32)],
        compiler_params=_params("parallel"),
        name="outproj_norm_router",
    )(att, ret, x2d, w_out_bf16, w_out_bf16, g.reshape(1, d), w_router_pad)


def _moe_kernel(xe_ref, wg_ref, wu_ref, wd_ref, gate_ref, o_ref):
    x = xe_ref[0]
    hg = jnp.dot(x, wg_ref[0], preferred_element_type=F32)
    hu = jnp.dot(x, wu_ref[0], preferred_element_type=F32)
    h = hg * (1.0 / (1.0 + jnp.exp(-hg))) * hu
    y = jnp.dot(h.astype(BF16), wd_ref[0], preferred_element_type=F32)
    o_ref[0] = (y * gate_ref[0]).astype(o_ref.dtype)


def expert_ffn(xe, gates, wg, wu, wd):
    e, cap, d = xe.shape
    ff = wg.shape[-1]
    tc = min(MOE_SLOT_TILE, cap)
    return pl.pallas_call(
        _moe_kernel,
        grid=(e, cap // tc),
        in_specs=[
            pl.BlockSpec((1, tc, d), lambda ei, t: (ei, t, 0)),
            pl.BlockSpec((1, d, ff), lambda ei, t: (ei, 0, 0)),
            pl.BlockSpec((1, d, ff), lambda ei, t: (ei, 0, 0)),
            pl.BlockSpec((1, ff, d), lambda ei, t: (ei, 0, 0)),
            pl.BlockSpec((1, tc, 1), lambda ei, t: (ei, t, 0)),
        ],
        out_specs=pl.BlockSpec((1, tc, d), lambda ei, t: (ei, t, 0)),
        out_shape=jax.ShapeDtypeStruct((e, cap, d), BF16),
        compiler_params=_params("parallel", "arbitrary"),
        name="expert_ffn",
    )(xe, wg, wu, wd, gates)


def _combine_kernel(lo_ref, hi_ref, x_ref, tok_ref, g_ref, ye_hbm, o_ref, buf, sem, *, tile, blk, apply_norm):
    i = pl.program_id(0)
    lo = lo_ref[i]
    hi = hi_ref[i]
    t0 = i * tile
    ahead = COMBINE_BUFFERS - 1

    def fetch(c):
        slot = lax.rem(c, COMBINE_BUFFERS)
        return pltpu.make_async_copy(ye_hbm.at[pl.ds(pl.multiple_of(c * blk, blk), blk)],
                                     buf.at[slot], sem.at[slot])

    def start_first_blocks(first, end):
        for k in range(ahead):
            @pl.when(first + k < end)
            def _():
                fetch(first + k).start()

    @pl.when(i == 0)
    def _():
        start_first_blocks(lo, hi)

    o_ref[...] = x_ref[...]
    row = lax.broadcasted_iota(jnp.int32, (tile, blk), 0)

    def body(c, carry):
        fetch(c).wait()

        @pl.when(c + ahead < hi)
        def _():
            fetch(c + ahead).start()

        rel = tok_ref[pl.ds(c, 1), :] - t0
        sel = jnp.where(row == rel, 1.0, 0.0).astype(BF16)
        o_ref[...] += jnp.dot(sel, buf[lax.rem(c, COMBINE_BUFFERS)], preferred_element_type=F32)
        return carry

    lax.fori_loop(lo, hi, body, 0)

    @pl.when(i + 1 < pl.num_programs(0))
    def _():
        start_first_blocks(lo_ref[i + 1], hi_ref[i + 1])

    if apply_norm:
        y = o_ref[...]
        var = jnp.mean(y * y, axis=-1, keepdims=True)
        o_ref[...] = y * lax.rsqrt(var + EPS) * g_ref[...]


def combine(x2d, ye_sorted, tok_sorted, final_g):
    n, d = x2d.shape
    rows = ye_sorted.shape[0]
    tile = min(COMBINE_TOKEN_TILE, n)
    blk = min(COMBINE_ROW_BLOCK, rows)
    bounds = jnp.searchsorted(tok_sorted, jnp.arange(0, n + 1, tile, dtype=jnp.int32)).astype(jnp.int32)
    start, end = bounds[:-1], bounds[1:]
    lo = start // blk
    hi = jnp.where(end > start, (end + blk - 1) // blk, lo)
    apply_norm = final_g is not None
    g = (final_g if apply_norm else jnp.ones((d,), F32)).reshape(1, d)
    return pl.pallas_call(
        functools.partial(_combine_kernel, tile=tile, blk=blk, apply_norm=apply_norm),
        grid=(n // tile,),
        in_specs=[pl.BlockSpec(memory_space=pltpu.SMEM), pl.BlockSpec(memory_space=pltpu.SMEM),
                  pl.BlockSpec((tile, d), lambda i: (i, 0)),
                  pl.BlockSpec((rows // blk, blk), lambda i: (0, 0)),
                  pl.BlockSpec((1, d), lambda i: (0, 0)),
                  pl.BlockSpec(memory_space=pl.ANY)],
        out_specs=pl.BlockSpec((tile, d), lambda i: (i, 0)),
        out_shape=jax.ShapeDtypeStruct((n, d), F32),
        scratch_shapes=[pltpu.VMEM((COMBINE_BUFFERS, blk, d), BF16),
                        pltpu.SemaphoreType.DMA((COMBINE_BUFFERS,))],
        compiler_params=_params("arbitrary"),
        name="combine",
    )(lo, hi, x2d, tok_sorted.reshape(rows // blk, blk), g, ye_sorted)


def _layer(x2d, bsz, seq, lw, band_bias, cos, sin, final_g):
    n = bsz * seq
    proj = norm_inproj(x2d, lw["norm_mix_g"], lw["w_in"]).reshape(bsz, seq, IN_WIDTH)
    att = window_attention(proj, band_bias, lw["attn_sink"]).reshape(n, ATT_WIDTH)
    ret = retention(proj, lw["log_decay"], cos, sin).reshape(n, RET_WIDTH)
    x2d, hn, aff = outproj_norm_router(att, ret, x2d, lw["w_out"], lw["norm_ffn_g"], lw["w_router"])
    cap = CAPACITY_FACTOR * n // N_EXPERTS
    gates, idx = lax.top_k(aff, cap)
    ye = expert_ffn(hn[idx], gates[..., None], lw["w_gate"], lw["w_up"], lw["w_down"])
    slots = N_EXPERTS * cap
    tok_sorted, perm = lax.sort_key_val(idx.reshape(slots).astype(jnp.int32),
                                        jnp.arange(slots, dtype=jnp.int32))
    return combine(x2d, ye.reshape(slots, D_MODEL)[perm], tok_sorted, final_g)


def kernel(x_prompt, x_sample, norm_mix_g, w_in, attn_sink, rel_bias, ret_decay_fwd, ret_decay_bwd,
           w_out, norm_ffn_g, w_router, w_gate, w_up, w_down, norm_final_g):
    depth = w_in.shape[0]
    band_bias = _band_bias(rel_bias)
    layers = []
    for l in range(depth):
        layers.append({
            "norm_mix_g": norm_mix_g[l],
            "w_in": w_in[l].astype(BF16),
            "attn_sink": attn_sink[l],
            "log_decay": jnp.stack([jax.nn.log_sigmoid(ret_decay_fwd[l].astype(F32)),
                                    jax.nn.log_sigmoid(ret_decay_bwd[l].astype(F32))]),
            "w_out": w_out[l].astype(BF16),
            "norm_ffn_g": norm_ffn_g[l],
            "w_router": jnp.pad(w_router[l].astype(BF16), ((0, 0), (0, LANES - N_EXPERTS))),
            "w_gate": w_gate[l].astype(BF16),
            "w_up": w_up[l].astype(BF16),
            "w_down": w_down[l].astype(BF16),
        })

    def trunk(x):
        bsz, seq, d = x.shape
        cos, sin = _rotary_tables(seq)
        x2d = x.reshape(bsz * seq, d)
        for l, lw in enumerate(layers):
            x2d = _layer(x2d, bsz, seq, lw, band_bias, cos, sin,
                         norm_final_g if l == depth - 1 else None)
        return x2d.reshape(bsz, seq, d)

    return (trunk(x_prompt), trunk(x_sample))
```

```python
import functools
import math

import jax
import jax.numpy as jnp
import numpy as np
from jax import lax
from jax.experimental import pallas as pl
from jax.experimental.pallas import tpu as pltpu

D_MODEL = 2048
HEAD_DIM_ATT = 128
N_Q_HEADS = 8
N_KV_HEADS = 2
GQA = N_Q_HEADS // N_KV_HEADS
ATT_WIDTH = N_Q_HEADS * HEAD_DIM_ATT
KV_WIDTH = N_KV_HEADS * HEAD_DIM_ATT
WINDOW = 128
BLOCK = 128
N_BUCKETS = 32
MAX_DISTANCE = 128
N_RET_HEADS = 4
RET_HEAD_DIM = 256
RET_WIDTH = N_RET_HEADS * RET_HEAD_DIM
RET_CHUNK = 256
D_MIX = ATT_WIDTH + RET_WIDTH
IN_WIDTH = ATT_WIDTH + 2 * KV_WIDTH + 4 * RET_WIDTH
N_EXPERTS = 16
CAPACITY_FACTOR = 2
EXPERT_FF = D_MODEL // 2
EPS = 1e-6
NEG_INF = -1e30
LANES = 128
LOG2_E = math.log2(math.e)

VMEM_LIMIT_BYTES = 56 * 1024 * 1024

ROW_TILE_INPROJ = 1024
COL_TILE_INPROJ = 1408
ROW_TILE_OUTPROJ = 512
ATT_Q_TILE = 512
RET_SEG = 2048
MOE_SLOT_TILE = 512
COMBINE_TOKEN_TILE = 256
CAST_BLOCK_ELEMS = 1024 * 1024
COMBINE_ROW_BLOCK = 256
COMBINE_BUFFERS = 4

BF16 = jnp.bfloat16
F32 = jnp.float32


def _params(*sem):
    return pltpu.CompilerParams(dimension_semantics=sem, vmem_limit_bytes=VMEM_LIMIT_BYTES)


def _cast_kernel(w_ref, o_ref):
    o_ref[...] = w_ref[0].astype(o_ref.dtype)


def layer_weight_bf16(w_stacked, layer):
    shape = w_stacked.shape[1:]
    cols = shape[-1]
    rows = math.prod(shape[:-1])
    tr = rows
    while tr * cols > CAST_BLOCK_ELEMS and tr % 32 == 0:
        tr //= 2
    out = pl.pallas_call(
        _cast_kernel,
        grid=(rows // tr,),
        in_specs=[pl.BlockSpec((1, tr, cols), lambda i: (layer, i, 0))],
        out_specs=pl.BlockSpec((tr, cols), lambda i: (i, 0)),
        out_shape=jax.ShapeDtypeStruct((rows, cols), BF16),
        compiler_params=_params("parallel"),
        name="weight_cast",
    )(w_stacked.reshape(w_stacked.shape[0], rows, cols))
    return out.reshape(shape)


def _norm_inproj_kernel(x_ref, g_ref, w_ref, o_ref, h_ref):
    @pl.when(pl.program_id(1) == 0)
    def _():
        x = x_ref[...]
        var = jnp.mean(x * x, axis=-1, keepdims=True)
        h_ref[...] = (x * lax.rsqrt(var + EPS) * g_ref[...]).astype(BF16)

    o_ref[...] = jnp.dot(h_ref[...], w_ref[...], preferred_element_type=F32).astype(o_ref.dtype)


def norm_inproj(x2d, g, w_bf16):
    n, d = x2d.shape
    width = w_bf16.shape[1]
    tm = min(ROW_TILE_INPROJ, n)
    tn = COL_TILE_INPROJ
    return pl.pallas_call(
        _norm_inproj_kernel,
        grid=(n // tm, width // tn),
        in_specs=[
            pl.BlockSpec((tm, d), lambda i, j: (i, 0)),
            pl.BlockSpec((1, d), lambda i, j: (0, 0)),
            pl.BlockSpec((d, tn), lambda i, j: (0, j)),
        ],
        out_specs=pl.BlockSpec((tm, tn), lambda i, j: (i, j)),
        out_shape=jax.ShapeDtypeStruct((n, width), BF16),
        scratch_shapes=[pltpu.VMEM((tm, d), BF16)],
        compiler_params=_params("parallel", "arbitrary"),
        name="norm_inproj",
    )(x2d, g.reshape(1, d), w_bf16)


def _t5_bucket(rel):
    half = N_BUCKETS // 2
    max_exact = half // 2
    bucket = jnp.where(rel > 0, half, 0)
    n = jnp.abs(rel)
    nf = jnp.maximum(n, 1).astype(F32)
    large = max_exact + (jnp.log(nf / max_exact) / math.log(MAX_DISTANCE / max_exact)
                         * (half - max_exact)).astype(jnp.int32)
    large = jnp.minimum(large, half - 1)
    return bucket + jnp.where(n < max_exact, n, large)


def _band_bias(rel_bias):
    qi = jnp.arange(BLOCK)[:, None]
    kj = jnp.arange(3 * BLOCK)[None, :]
    rel = kj - BLOCK - qi
    onehot = jax.nn.one_hot(_t5_bucket(rel), N_BUCKETS, dtype=F32)
    bias = jnp.einsum('qkb,bh->qkh', onehot, rel_bias.astype(F32),
                      precision=lax.Precision.HIGHEST)
    bias = jnp.where((jnp.abs(rel) <= WINDOW)[:, :, None], bias, NEG_INF)
    bias = bias.reshape(BLOCK, 3, BLOCK, N_KV_HEADS, GQA).transpose(3, 1, 2, 4, 0)
    bias = bias.reshape(N_KV_HEADS, 3, BLOCK, GQA * BLOCK)
    return jnp.concatenate([bias, jnp.full_like(bias[:, :1], NEG_INF)], axis=1) * LOG2_E


def _attn_kernel(q_ref, kp_ref, km_ref, kn_ref, vp_ref, vm_ref, vn_ref, bias_ref, sink_ref,
                 o_ref, k_scr, v_scr, *, tq):
    i = pl.program_id(1)
    last = pl.num_programs(1) - 1
    k_scr[0:BLOCK] = kp_ref[0]
    k_scr[BLOCK:BLOCK + tq] = km_ref[0]
    k_scr[BLOCK + tq:] = kn_ref[0]
    v_scr[0:BLOCK] = vp_ref[0]
    v_scr[BLOCK:BLOCK + tq] = vm_ref[0]
    v_scr[BLOCK + tq:] = vn_ref[0]
    scale = HEAD_DIM_ATT ** -0.5 * LOG2_E
    nsb = tq // BLOCK
    nt = (((1,), (1,)), ((), ()))
    tn = (((0,), (0,)), ((), ()))
    for sb in range(nsb):
        for h in range(N_KV_HEADS):
            c0 = h * HEAD_DIM_ATT
            qs = jnp.concatenate(
                [q_ref[0, sb * BLOCK:(sb + 1) * BLOCK,
                       (h * GQA + g) * HEAD_DIM_ATT:(h * GQA + g + 1) * HEAD_DIM_ATT]
                 for g in range(GQA)], axis=0)
            parts = []
            for c in range(3):
                kc = k_scr[(sb + c) * BLOCK:(sb + c + 1) * BLOCK, c0:c0 + HEAD_DIM_ATT]
                if c == 0 and sb == 0:
                    slot = jnp.where(i == 0, 3, 0)
                elif c == 2 and sb == nsb - 1:
                    slot = jnp.where(i == last, 3, 2)
                else:
                    slot = c
                s_c = lax.dot_general(kc, qs, nt, preferred_element_type=F32)
                parts.append(s_c * scale + bias_ref[h, slot])
            s = jnp.concatenate(parts, axis=0)
            sink = sink_ref[h:h + 1, :]
            m = jnp.maximum(jnp.max(s, axis=0, keepdims=True), sink)
            p = jnp.exp2(s - m)
            denom = jnp.sum(p, axis=0, keepdims=True) + jnp.exp2(sink - m)
            pn = (p * (1.0 / denom)).astype(BF16)
            vb = v_scr[sb * BLOCK:(sb + 3) * BLOCK, c0:c0 + HEAD_DIM_ATT]
            o = lax.dot_general(pn, vb, tn, preferred_element_type=F32)
            for g in range(GQA):
                o_ref[0, sb * BLOCK:(sb + 1) * BLOCK,
                      (h * GQA + g) * HEAD_DIM_ATT:(h * GQA + g + 1) * HEAD_DIM_ATT] = (
                    o[g * BLOCK:(g + 1) * BLOCK].astype(o_ref.dtype))


def window_attention(proj, band_bias, sink):
    b, s, _ = proj.shape
    tq = min(ATT_Q_TILE, s)
    nblk = s // BLOCK
    r = tq // BLOCK
    assert r >= 2 and s % tq == 0
    k_col = ATT_WIDTH // KV_WIDTH
    v_col = k_col + 1

    def main(col):
        return pl.BlockSpec((1, tq, KV_WIDTH), lambda bi, i: (bi, i, col))

    def prev(col):
        return pl.BlockSpec((1, BLOCK, KV_WIDTH), lambda bi, i: (bi, jnp.maximum(i * r - 1, 0), col))

    def nxt(col):
        return pl.BlockSpec((1, BLOCK, KV_WIDTH),
                            lambda bi, i: (bi, jnp.minimum((i + 1) * r, nblk - 1), col))

    sink_rows = jnp.repeat(sink.astype(F32) * LOG2_E, BLOCK).reshape(N_KV_HEADS, GQA * BLOCK)
    return pl.pallas_call(
        functools.partial(_attn_kernel, tq=tq),
        grid=(b, s // tq),
        in_specs=[
            pl.BlockSpec((1, tq, ATT_WIDTH), lambda bi, i: (bi, i, 0)),
            prev(k_col), main(k_col), nxt(k_col),
            prev(v_col), main(v_col), nxt(v_col),
            pl.BlockSpec((N_KV_HEADS, 4, BLOCK, GQA * BLOCK), lambda bi, i: (0, 0, 0, 0)),
            pl.BlockSpec((N_KV_HEADS, GQA * BLOCK), lambda bi, i: (0, 0)),
        ],
        out_specs=pl.BlockSpec((1, tq, ATT_WIDTH), lambda bi, i: (bi, i, 0)),
        out_shape=jax.ShapeDtypeStruct((b, s, ATT_WIDTH), BF16),
        scratch_shapes=[pltpu.VMEM((tq + 2 * BLOCK, KV_WIDTH), BF16),
                        pltpu.VMEM((tq + 2 * BLOCK, KV_WIDTH), BF16)],
        compiler_params=_params("parallel", "arbitrary"),
        name="window_attention",
    )(proj, proj, proj, proj, proj, proj, proj, band_bias, sink_rows)


def _rotate(x, cos, sin):
    half = RET_HEAD_DIM // 2
    x1 = x[:, :half]
    x2 = x[:, half:]
    return jnp.concatenate([x1 * cos - x2 * sin, x1 * sin + x2 * cos], axis=-1)


def _ret_kernel(ld_ref, q_ref, k_ref, v_ref, g_ref, cos_ref, sin_ref, o_ref,
                oacc, qrot, kbwd, kv_scr, rprev, rstate, *, nseg, seg_len):
    h = pl.program_id(1)
    s = pl.program_id(2)
    cs = RET_CHUNK
    nchunk = seg_len // cs
    ld_f = ld_ref[0, h]
    ld_b = ld_ref[1, h]
    ii = lax.broadcasted_iota(jnp.int32, (cs, 1), 0).astype(F32)
    jj = lax.broadcasted_iota(jnp.int32, (1, cs), 1).astype(F32)
    k_scale = RET_HEAD_DIM ** -0.5
    nt = (((1,), (1,)), ((), ()))
    tn = (((0,), (0,)), ((), ()))
    chunk_len = jnp.full((1, 1), cs, F32)

    @pl.when(s == 0)
    def _():
        rstate[...] = jnp.zeros_like(rstate)

    @pl.when(s < nseg)
    def _():
        base = s * seg_len
        diff = ii - jj
        dmat = k_scale * jnp.where(diff >= 0, jnp.exp(ld_f * jnp.maximum(diff, 0.0)),
                                   jnp.exp(ld_b * jnp.maximum(-diff, 0.0)))
        q_dec = jnp.exp(ld_f * (ii + 1.0))
        k_dec_f = k_scale * jnp.exp(ld_f * (cs - 1.0 - ii))
        k_dec_b = k_scale * jnp.exp(ld_b * ii)
        g_chunk = jnp.exp(chunk_len * ld_f)

        def products(c, carry):
            r0 = pl.multiple_of(c * cs, cs)
            rows = pl.ds(pl.multiple_of(base + r0, cs), cs)
            cos = cos_ref[pl.ds(r0, cs), :]
            sin = sin_ref[pl.ds(r0, cs), :]
            q = _rotate(q_ref[0, pl.ds(r0, cs), :].astype(F32), cos, sin).astype(BF16)
            k = _rotate(k_ref[0, pl.ds(r0, cs), :].astype(F32), cos, sin)
            v = v_ref[0, pl.ds(r0, cs), :]
            qrot[rows, :] = q
            kbwd[rows, :] = (k * k_dec_b).astype(BF16)
            a = lax.dot_general(q, k.astype(BF16), nt, preferred_element_type=F32) * dmat
            oacc[rows, :] = jnp.dot(a.astype(BF16), v, preferred_element_type=F32)
            kv_scr[c] = lax.dot_general((k * k_dec_f).astype(BF16), v, tn, preferred_element_type=F32)
            return carry

        for c_static in range(nchunk):
            products(c_static, 0)

        def scan(c, carry):
            r = rstate[0]
            rprev[c] = r.astype(BF16)
            rstate[0] = g_chunk * r + kv_scr[c]
            return carry

        for c_static in range(nchunk):
            scan(c_static, 0)

        def cross(c, carry):
            rows = pl.ds(pl.multiple_of(base + c * cs, cs), cs)
            oacc[rows, :] += q_dec * jnp.dot(qrot[rows, :], rprev[c], preferred_element_type=F32)
            return carry

        for c_static in range(nchunk):
            cross(c_static, 0)

    @pl.when(s >= nseg)
    def _():
        base = (2 * nseg - 1 - s) * seg_len
        q_dec = jnp.exp(ld_b * (cs - ii))
        g_chunk = jnp.exp(chunk_len * ld_b)

        def products(c, carry):
            r0 = pl.multiple_of(c * cs, cs)
            rows = pl.ds(pl.multiple_of(base + r0, cs), cs)
            kv_scr[c] = lax.dot_general(kbwd[rows, :], v_ref[0, pl.ds(r0, cs), :], tn,
                                        preferred_element_type=F32)
            return carry

        for c_static in range(nchunk):
            products(c_static, 0)

        def scan(t, carry):
            c = nchunk - 1 - t
            r = rstate[1]
            rprev[c] = r.astype(BF16)
            rstate[1] = g_chunk * r + kv_scr[c]
            return carry

        for c_static in range(nchunk):
            scan(c_static, 0)

        def finish(c, carry):
            r0 = pl.multiple_of(c * cs, cs)
            rows = pl.ds(pl.multiple_of(base + r0, cs), cs)
            o = oacc[rows, :] + q_dec * jnp.dot(qrot[rows, :], rprev[c], preferred_element_type=F32)
            mu = jnp.mean(o, axis=-1, keepdims=True)
            oc = o - mu
            var = jnp.mean(oc * oc, axis=-1, keepdims=True)
            gate = g_ref[0, pl.ds(r0, cs), :].astype(F32)
            silu = gate * (1.0 / (1.0 + jnp.exp(-gate)))
            o_ref[0, pl.ds(r0, cs), :] = (oc * lax.rsqrt(var + EPS) * silu).astype(o_ref.dtype)
            return carry

        for c_static in range(nchunk):
            finish(c_static, 0)


def _rotary_tables(seq_len):
    half = RET_HEAD_DIM // 2
    inv = 1.0 / (10000.0 ** jnp.linspace(0.0, 1.0, half, dtype=F32))
    ang = jnp.arange(seq_len, dtype=F32)[:, None] * inv[None, :]
    return jnp.cos(ang), jnp.sin(ang)


def retention(proj, log_decay, cos, sin):
    b, s, _ = proj.shape
    seg_len = min(RET_SEG, s)
    nseg = s // seg_len
    nchunk = seg_len // RET_CHUNK
    base = (ATT_WIDTH + 2 * KV_WIDTH) // RET_HEAD_DIM

    def seg_both(t):
        return jnp.where(t < nseg, t, 2 * nseg - 1 - t)

    def seg_fwd_only(t):
        return jnp.minimum(t, nseg - 1)

    def seg_bwd_only(t):
        return jnp.where(t < nseg, nseg - 1, 2 * nseg - 1 - t)

    def head_block(which, seg):
        off = base + which * N_RET_HEADS
        return pl.BlockSpec((1, seg_len, RET_HEAD_DIM), lambda bi, h, t: (bi, seg(t), off + h))

    table = pl.BlockSpec((seg_len, RET_HEAD_DIM // 2), lambda bi, h, t: (seg_fwd_only(t), 0))
    return pl.pallas_call(
        functools.partial(_ret_kernel, nseg=nseg, seg_len=seg_len),
        grid=(b, N_RET_HEADS, 2 * nseg),
        in_specs=[pl.BlockSpec(memory_space=pltpu.SMEM),
                  head_block(0, seg_fwd_only), head_block(1, seg_fwd_only),
                  head_block(2, seg_both), head_block(3, seg_bwd_only), table, table],
        out_specs=pl.BlockSpec((1, seg_len, RET_HEAD_DIM),
                               lambda bi, h, t: (bi, seg_bwd_only(t), h)),
        out_shape=jax.ShapeDtypeStruct((b, s, RET_WIDTH), BF16),
        scratch_shapes=[pltpu.VMEM((s, RET_HEAD_DIM), F32),
                        pltpu.VMEM((s, RET_HEAD_DIM), BF16),
                        pltpu.VMEM((s, RET_HEAD_DIM), BF16),
                        pltpu.VMEM((nchunk, RET_HEAD_DIM, RET_HEAD_DIM), F32),
                        pltpu.VMEM((nchunk, RET_HEAD_DIM, RET_HEAD_DIM), BF16),
                        pltpu.VMEM((2, RET_HEAD_DIM, RET_HEAD_DIM), F32)],
        compiler_params=_params("parallel", "arbitrary", "arbitrary"),
        name="retention",
    )(log_decay, proj, proj, proj, proj, cos, sin)


def _outproj_kernel(att_ref, ret_ref, x_ref, wa_ref, wr_ref, g_ref, wrt_ref, xo_ref, hn_ref, aff_ref):
    y = x_ref[...] + jnp.dot(att_ref[...], wa_ref[...], preferred_element_type=F32)
    y = y + jnp.dot(ret_ref[...], wr_ref[...], preferred_element_type=F32)
    xo_ref[...] = y
    var = jnp.mean(y * y, axis=-1, keepdims=True)
    hn = (y * lax.rsqrt(var + EPS) * g_ref[...]).astype(BF16)
    hn_ref[...] = hn
    logits = jnp.dot(hn, wrt_ref[...], preferred_element_type=F32)
    lane = lax.broadcasted_iota(jnp.int32, (1, LANES), 1)
    logits = jnp.where(lane < N_EXPERTS, logits, NEG_INF)
    e = jnp.exp(logits - jnp.max(logits, axis=-1, keepdims=True))
    aff = e / jnp.sum(e, axis=-1, keepdims=True)
    aff_ref[...] = jnp.transpose(aff)[:N_EXPERTS]


def outproj_norm_router(att, ret, x2d, w_out_bf16, g, w_router_pad):
    n, d = x2d.shape
    tm = min(ROW_TILE_OUTPROJ, n)
    row = lambda i: (i, 0)
    return pl.pallas_call(
        _outproj_kernel,
        grid=(n // tm,),
        in_specs=[
            pl.BlockSpec((tm, ATT_WIDTH), row),
            pl.BlockSpec((tm, RET_WIDTH), row),
            pl.BlockSpec((tm, d), row),
            pl.BlockSpec((ATT_WIDTH, d), lambda i: (0, 0)),
            pl.BlockSpec((RET_WIDTH, d), lambda i: (1, 0)),
            pl.BlockSpec((1, d), lambda i: (0, 0)),
            pl.BlockSpec((d, LANES), lambda i: (0, 0)),
        ],
        out_specs=[pl.BlockSpec((tm, d), row), pl.BlockSpec((tm, d), row),
                   pl.BlockSpec((N_EXPERTS, tm), lambda i: (0, i))],
        out_shape=[jax.ShapeDtypeStruct((n, d), F32), jax.ShapeDtypeStruct((n, d), BF16),
                   jax.ShapeDtypeStruct((N_EXPERTS, n), F32)],
        compiler_params=_params("parallel"),
        name="outproj_norm_router",
    )(att, ret, x2d, w_out_bf16, w_out_bf16, g.reshape(1, d), w_router_pad)


def _moe_kernel(xe_ref, wg_ref, wu_ref, wd_ref, gate_ref, o_ref):
    x = xe_ref[0]
    hg = jnp.dot(x, wg_ref[0], preferred_element_type=F32)
    hu = jnp.dot(x, wu_ref[0], preferred_element_type=F32)
    h = hg * (1.0 / (1.0 + jnp.exp(-hg))) * hu
    y = jnp.dot(h.astype(BF16), wd_ref[0], preferred_element_type=F32)
    o_ref[0] = (y * gate_ref[0]).astype(o_ref.dtype)


def expert_ffn(xe, gates, wg, wu, wd):
    e, cap, d = xe.shape
    ff = wg.shape[-1]
    tc = min(MOE_SLOT_TILE, cap)
    return pl.pallas_call(
        _moe_kernel,
        grid=(e, cap // tc),
        in_specs=[
            pl.BlockSpec((1, tc, d), lambda ei, t: (ei, t, 0)),
            pl.BlockSpec((1, d, ff), lambda ei, t: (ei, 0, 0)),
            pl.BlockSpec((1, d, ff), lambda ei, t: (ei, 0, 0)),
            pl.BlockSpec((1, ff, d), lambda ei, t: (ei, 0, 0)),
            pl.BlockSpec((1, tc, 1), lambda ei, t: (ei, t, 0)),
        ],
        out_specs=pl.BlockSpec((1, tc, d), lambda ei, t: (ei, t, 0)),
        out_shape=jax.ShapeDtypeStruct((e, cap, d), BF16),
        compiler_params=_params("parallel", "arbitrary"),
        name="expert_ffn",
    )(xe, wg, wu, wd, gates)


def _combine_kernel(lo_ref, hi_ref, x_ref, tok_ref, g_ref, ye_hbm, o_ref, buf, sem, *, tile, blk, apply_norm):
    i = pl.program_id(0)
    lo = lo_ref[i]
    hi = hi_ref[i]
    t0 = i * tile
    ahead = COMBINE_BUFFERS - 1

    def fetch(c):
        slot = lax.rem(c, COMBINE_BUFFERS)
        return pltpu.make_async_copy(ye_hbm.at[pl.ds(pl.multiple_of(c * blk, blk), blk)],
                                     buf.at[slot], sem.at[slot])

    def start_first_blocks(first, end):
        for k in range(ahead):
            @pl.when(first + k < end)
            def _():
                fetch(first + k).start()

    @pl.when(i == 0)
    def _():
        start_first_blocks(lo, hi)

    o_ref[...] = x_ref[...]
    row = lax.broadcasted_iota(jnp.int32, (tile, blk), 0)

    def body(c, carry):
        fetch(c).wait()

        @pl.when(c + ahead < hi)
        def _():
            fetch(c + ahead).start()

        rel = tok_ref[pl.ds(c, 1), :] - t0
        sel = jnp.where(row == rel, 1.0, 0.0).astype(BF16)
        o_ref[...] += jnp.dot(sel, buf[lax.rem(c, COMBINE_BUFFERS)], preferred_element_type=F32)
        return carry

    lax.fori_loop(lo, hi, body, 0)

    @pl.when(i + 1 < pl.num_programs(0))
    def _():
        start_first_blocks(lo_ref[i + 1], hi_ref[i + 1])

    if apply_norm:
        y = o_ref[...]
        var = jnp.mean(y * y, axis=-1, keepdims=True)
        o_ref[...] = y * lax.rsqrt(var + EPS) * g_ref[...]


def combine(x2d, ye_sorted, tok_sorted, final_g):
    n, d = x2d.shape
    rows = ye_sorted.shape[0]
    tile = min(COMBINE_TOKEN_TILE, n)
    blk = min(COMBINE_ROW_BLOCK, rows)
    bounds = jnp.searchsorted(tok_sorted, jnp.arange(0, n + 1, tile, dtype=jnp.int32)).astype(jnp.int32)
    start, end = bounds[:-1], bounds[1:]
    lo = start // blk
    hi = jnp.where(end > start, (end + blk - 1) // blk, lo)
    apply_norm = final_g is not None
    g = (final_g if apply_norm else jnp.ones((d,), F32)).reshape(1, d)
    return pl.pallas_call(
        functools.partial(_combine_kernel, tile=tile, blk=blk, apply_norm=apply_norm),
        grid=(n // tile,),
        in_specs=[pl.BlockSpec(memory_space=pltpu.SMEM), pl.BlockSpec(memory_space=pltpu.SMEM),
                  pl.BlockSpec((tile, d), lambda i: (i, 0)),
                  pl.BlockSpec((rows // blk, blk), lambda i: (0, 0)),
                  pl.BlockSpec((1, d), lambda i: (0, 0)),
                  pl.BlockSpec(memory_space=pl.ANY)],
        out_specs=pl.BlockSpec((tile, d), lambda i: (i, 0)),
        out_shape=jax.ShapeDtypeStruct((n, d), F32),
        scratch_shapes=[pltpu.VMEM((COMBINE_BUFFERS, blk, d), BF16),
                        pltpu.SemaphoreType.DMA((COMBINE_BUFFERS,))],
        compiler_params=_params("arbitrary"),
        name="combine",
    )(lo, hi, x2d, tok_sorted.reshape(rows // blk, blk), g, ye_sorted)


def _layer(x2d, bsz, seq, lw, band_bias, cos, sin, final_g):
    n = bsz * seq
    proj = norm_inproj(x2d, lw["norm_mix_g"], lw["w_in"]).reshape(bsz, seq, IN_WIDTH)
    att = window_attention(proj, band_bias, lw["attn_sink"]).reshape(n, ATT_WIDTH)
    ret = retention(proj, lw["log_decay"], cos, sin).reshape(n, RET_WIDTH)
    x2d, hn, aff = outproj_norm_router(att, ret, x2d, lw["w_out"], lw["norm_ffn_g"], lw["w_router"])
    cap = CAPACITY_FACTOR * n // N_EXPERTS
    gates, idx = lax.top_k(aff, cap)
    ye = expert_ffn(hn[idx], gates[..., None], lw["w_gate"], lw["w_up"], lw["w_down"])
    slots = N_EXPERTS * cap
    tok_sorted, perm = lax.sort_key_val(idx.reshape(slots).astype(jnp.int32),
                                        jnp.arange(slots, dtype=jnp.int32))
    return combine(x2d, ye.reshape(slots, D_MODEL)[perm], tok_sorted, final_g)


def kernel(x_prompt, x_sample, norm_mix_g, w_in, attn_sink, rel_bias, ret_decay_fwd, ret_decay_bwd,
           w_out, norm_ffn_g, w_router, w_gate, w_up, w_down, norm_final_g):
    depth = w_in.shape[0]
    band_bias = _band_bias(rel_bias)
    layers = []
    for l in range(depth):
        layers.append({
            "norm_mix_g": norm_mix_g[l],
            "w_in": layer_weight_bf16(w_in, l),
            "attn_sink": attn_sink[l],
            "log_decay": jnp.stack([jax.nn.log_sigmoid(ret_decay_fwd[l].astype(F32)),
                                    jax.nn.log_sigmoid(ret_decay_bwd[l].astype(F32))]),
            "w_out": layer_weight_bf16(w_out, l),
            "norm_ffn_g": norm_ffn_g[l],
            "w_router": jnp.pad(w_router[l].astype(BF16), ((0, 0), (0, LANES - N_EXPERTS))),
            "w_gate": layer_weight_bf16(w_gate, l),
            "w_up": layer_weight_bf16(w_up, l),
            "w_down": layer_weight_bf16(w_down, l),
        })

    def trunk(x):
        bsz, seq, d = x.shape
        cos, sin = _rotary_tables(seq)
        x2d = x.reshape(bsz * seq, d)
        for l, lw in enumerate(layers):
            x2d = _layer(x2d, bsz, seq, lw, band_bias, cos, sin,
                         norm_final_g if l == depth - 1 else None)
        return x2d.reshape(bsz, seq, d)

    return (trunk(x_prompt), trunk(x_sample))
```

```python
import functools
import math

import jax
import jax.numpy as jnp
import numpy as np
from jax import lax
from jax.experimental import pallas as pl
from jax.experimental.pallas import tpu as pltpu

D_MODEL = 2048
HEAD_DIM_ATT = 128
N_Q_HEADS = 8
N_KV_HEADS = 2
GQA = N_Q_HEADS // N_KV_HEADS
ATT_WIDTH = N_Q_HEADS * HEAD_DIM_ATT
KV_WIDTH = N_KV_HEADS * HEAD_DIM_ATT
WINDOW = 128
BLOCK = 128
N_BUCKETS = 32
MAX_DISTANCE = 128
N_RET_HEADS = 4
RET_HEAD_DIM = 256
RET_WIDTH = N_RET_HEADS * RET_HEAD_DIM
RET_CHUNK = 256
D_MIX = ATT_WIDTH + RET_WIDTH
IN_WIDTH = ATT_WIDTH + 2 * KV_WIDTH + 4 * RET_WIDTH
N_EXPERTS = 16
CAPACITY_FACTOR = 2
EXPERT_FF = D_MODEL // 2
EPS = 1e-6
NEG_INF = -1e30
LANES = 128
LOG2_E = math.log2(math.e)

VMEM_LIMIT_BYTES = 56 * 1024 * 1024

ROW_TILE_INPROJ = 1024
COL_TILE_INPROJ = 1408
ROW_TILE_OUTPROJ = 512
ATT_Q_TILE = 512
RET_SEG = 2048
MOE_SLOT_TILE = 512
COMBINE_TOKEN_TILE = 256
CAST_BLOCK_ELEMS = 1024 * 1024
SIDE_CAST_BLOCK_ELEMS = 512 * 1024
COMBINE_ROW_BLOCK = 256
COMBINE_BUFFERS = 4

BF16 = jnp.bfloat16
F32 = jnp.float32


def _params(*sem):
    return pltpu.CompilerParams(dimension_semantics=sem, vmem_limit_bytes=VMEM_LIMIT_BYTES)


def _cast_kernel(w_ref, o_ref):
    o_ref[...] = w_ref[0].astype(o_ref.dtype)


def layer_weight_bf16(w_stacked, layer):
    shape = w_stacked.shape[1:]
    cols = shape[-1]
    rows = math.prod(shape[:-1])
    tr = rows
    while tr * cols > CAST_BLOCK_ELEMS and tr % 32 == 0:
        tr //= 2
    out = pl.pallas_call(
        _cast_kernel,
        grid=(rows // tr,),
        in_specs=[pl.BlockSpec((1, tr, cols), lambda i: (layer, i, 0))],
        out_specs=pl.BlockSpec((tr, cols), lambda i: (i, 0)),
        out_shape=jax.ShapeDtypeStruct((rows, cols), BF16),
        compiler_params=_params("parallel"),
        name="weight_cast",
    )(w_stacked.reshape(w_stacked.shape[0], rows, cols))
    return out.reshape(shape)


def _norm_inproj_kernel(x_ref, g_ref, w_ref, *rest, n_cast):
    cast_in = rest[:n_cast]
    o_ref = rest[n_cast]
    cast_out = rest[n_cast + 1:2 * n_cast + 1]
    h_ref = rest[2 * n_cast + 1]

    @pl.when(pl.program_id(1) == 0)
    def _():
        x = x_ref[...]
        var = jnp.mean(x * x, axis=-1, keepdims=True)
        h_ref[...] = (x * lax.rsqrt(var + EPS) * g_ref[...]).astype(BF16)

    o_ref[...] = jnp.dot(h_ref[...], w_ref[...], preferred_element_type=F32).astype(o_ref.dtype)
    for src, dst in zip(cast_in, cast_out):
        dst[...] = src[0].astype(dst.dtype)


def side_cast_fits(n, width, stacked):
    steps = (n // min(ROW_TILE_INPROJ, n)) * (width // COL_TILE_INPROJ)
    rows = math.prod(stacked.shape[1:-1])
    per_step = rows // steps
    return (rows % steps == 0 and per_step % 16 == 0
            and per_step * stacked.shape[-1] <= SIDE_CAST_BLOCK_ELEMS)


def norm_inproj(x2d, g, w_bf16, cast_stacks=(), cast_layer=0):
    n, d = x2d.shape
    width = w_bf16.shape[1]
    tm = min(ROW_TILE_INPROJ, n)
    tn = COL_TILE_INPROJ
    ncol = width // tn
    steps = (n // tm) * ncol
    in_specs = [
        pl.BlockSpec((tm, d), lambda i, j: (i, 0)),
        pl.BlockSpec((1, d), lambda i, j: (0, 0)),
        pl.BlockSpec((d, tn), lambda i, j: (0, j)),
    ]
    out_specs = [pl.BlockSpec((tm, tn), lambda i, j: (i, j))]
    out_shape = [jax.ShapeDtypeStruct((n, width), BF16)]
    cast_args = []
    for st in cast_stacks:
        cols = st.shape[-1]
        rows = math.prod(st.shape[1:-1])
        per_step = rows // steps
        in_specs.append(pl.BlockSpec((1, per_step, cols), lambda i, j: (cast_layer, i * ncol + j, 0)))
        out_specs.append(pl.BlockSpec((per_step, cols), lambda i, j: (i * ncol + j, 0)))
        out_shape.append(jax.ShapeDtypeStruct((rows, cols), BF16))
        cast_args.append(st.reshape(st.shape[0], rows, cols))
    outs = pl.pallas_call(
        functools.partial(_norm_inproj_kernel, n_cast=len(cast_stacks)),
        grid=(n // tm, ncol),
        in_specs=in_specs,
        out_specs=out_specs,
        out_shape=out_shape,
        scratch_shapes=[pltpu.VMEM((tm, d), BF16)],
        compiler_params=_params("parallel", "arbitrary"),
        name="norm_inproj",
    )(x2d, g.reshape(1, d), w_bf16, *cast_args)
    return outs[0], [o.reshape(st.shape[1:]) for o, st in zip(outs[1:], cast_stacks)]


def _t5_bucket(rel):
    half = N_BUCKETS // 2
    max_exact = half // 2
    bucket = jnp.where(rel > 0, half, 0)
    n = jnp.abs(rel)
    nf = jnp.maximum(n, 1).astype(F32)
    large = max_exact + (jnp.log(nf / max_exact) / math.log(MAX_DISTANCE / max_exact)
                         * (half - max_exact)).astype(jnp.int32)
    large = jnp.minimum(large, half - 1)
    return bucket + jnp.where(n < max_exact, n, large)


def _band_bias(rel_bias):
    qi = jnp.arange(BLOCK)[:, None]
    kj = jnp.arange(3 * BLOCK)[None, :]
    rel = kj - BLOCK - qi
    onehot = jax.nn.one_hot(_t5_bucket(rel), N_BUCKETS, dtype=F32)
    bias = jnp.einsum('qkb,bh->qkh', onehot, rel_bias.astype(F32),
                      precision=lax.Precision.HIGHEST)
    bias = jnp.where((jnp.abs(rel) <= WINDOW)[:, :, None], bias, NEG_INF)
    bias = bias.reshape(BLOCK, 3, BLOCK, N_KV_HEADS, GQA).transpose(3, 1, 2, 4, 0)
    bias = bias.reshape(N_KV_HEADS, 3, BLOCK, GQA * BLOCK)
    return jnp.concatenate([bias, jnp.full_like(bias[:, :1], NEG_INF)], axis=1) * LOG2_E


def _attn_kernel(q_ref, kp_ref, km_ref, kn_ref, vp_ref, vm_ref, vn_ref, bias_ref, sink_ref,
                 o_ref, k_scr, v_scr, *, tq):
    i = pl.program_id(1)
    last = pl.num_programs(1) - 1
    k_scr[0:BLOCK] = kp_ref[0]
    k_scr[BLOCK:BLOCK + tq] = km_ref[0]
    k_scr[BLOCK + tq:] = kn_ref[0]
    v_scr[0:BLOCK] = vp_ref[0]
    v_scr[BLOCK:BLOCK + tq] = vm_ref[0]
    v_scr[BLOCK + tq:] = vn_ref[0]
    scale = HEAD_DIM_ATT ** -0.5 * LOG2_E
    nsb = tq // BLOCK
    nt = (((1,), (1,)), ((), ()))
    tn = (((0,), (0,)), ((), ()))
    for sb in range(nsb):
        for h in range(N_KV_HEADS):
            c0 = h * HEAD_DIM_ATT
            qs = jnp.concatenate(
                [q_ref[0, sb * BLOCK:(sb + 1) * BLOCK,
                       (h * GQA + g) * HEAD_DIM_ATT:(h * GQA + g + 1) * HEAD_DIM_ATT]
                 for g in range(GQA)], axis=0)
            parts = []
            for c in range(3):
                kc = k_scr[(sb + c) * BLOCK:(sb + c + 1) * BLOCK, c0:c0 + HEAD_DIM_ATT]
                if c == 0 and sb == 0:
                    slot = jnp.where(i == 0, 3, 0)
                elif c == 2 and sb == nsb - 1:
                    slot = jnp.where(i == last, 3, 2)
                else:
                    slot = c
                s_c = lax.dot_general(kc, qs, nt, preferred_element_type=F32)
                parts.append(s_c * scale + bias_ref[h, slot])
            s = jnp.concatenate(parts, axis=0)
            sink = sink_ref[h:h + 1, :]
            m = jnp.maximum(jnp.max(s, axis=0, keepdims=True), sink)
            p = jnp.exp2(s - m)
            denom = jnp.sum(p, axis=0, keepdims=True) + jnp.exp2(sink - m)
            pn = (p * (1.0 / denom)).astype(BF16)
            vb = v_scr[sb * BLOCK:(sb + 3) * BLOCK, c0:c0 + HEAD_DIM_ATT]
            o = lax.dot_general(pn, vb, tn, preferred_element_type=F32)
            for g in range(GQA):
                o_ref[0, sb * BLOCK:(sb + 1) * BLOCK,
                      (h * GQA + g) * HEAD_DIM_ATT:(h * GQA + g + 1) * HEAD_DIM_ATT] = (
                    o[g * BLOCK:(g + 1) * BLOCK].astype(o_ref.dtype))


def window_attention(proj, band_bias, sink):
    b, s, _ = proj.shape
    tq = min(ATT_Q_TILE, s)
    nblk = s // BLOCK
    r = tq // BLOCK
    assert r >= 2 and s % tq == 0
    k_col = ATT_WIDTH // KV_WIDTH
    v_col = k_col + 1

    def main(col):
        return pl.BlockSpec((1, tq, KV_WIDTH), lambda bi, i: (bi, i, col))

    def prev(col):
        return pl.BlockSpec((1, BLOCK, KV_WIDTH), lambda bi, i: (bi, jnp.maximum(i * r - 1, 0), col))

    def nxt(col):
        return pl.BlockSpec((1, BLOCK, KV_WIDTH),
                            lambda bi, i: (bi, jnp.minimum((i + 1) * r, nblk - 1), col))

    sink_rows = jnp.repeat(sink.astype(F32) * LOG2_E, BLOCK).reshape(N_KV_HEADS, GQA * BLOCK)
    return pl.pallas_call(
        functools.partial(_attn_kernel, tq=tq),
        grid=(b, s // tq),
        in_specs=[
            pl.BlockSpec((1, tq, ATT_WIDTH), lambda bi, i: (bi, i, 0)),
            prev(k_col), main(k_col), nxt(k_col),
            prev(v_col), main(v_col), nxt(v_col),
            pl.BlockSpec((N_KV_HEADS, 4, BLOCK, GQA * BLOCK), lambda bi, i: (0, 0, 0, 0)),
            pl.BlockSpec((N_KV_HEADS, GQA * BLOCK), lambda bi, i: (0, 0)),
        ],
        out_specs=pl.BlockSpec((1, tq, ATT_WIDTH), lambda bi, i: (bi, i, 0)),
        out_shape=jax.ShapeDtypeStruct((b, s, ATT_WIDTH), BF16),
        scratch_shapes=[pltpu.VMEM((tq + 2 * BLOCK, KV_WIDTH), BF16),
                        pltpu.VMEM((tq + 2 * BLOCK, KV_WIDTH), BF16)],
        compiler_params=_params("parallel", "arbitrary"),
        name="window_attention",
    )(proj, proj, proj, proj, proj, proj, proj, band_bias, sink_rows)


def _rotate(x, cos, sin):
    half = RET_HEAD_DIM // 2
    x1 = x[:, :half]
    x2 = x[:, half:]
    return jnp.concatenate([x1 * cos - x2 * sin, x1 * sin + x2 * cos], axis=-1)


def _ret_kernel(ld_ref, q_ref, k_ref, v_ref, g_ref, cos_ref, sin_ref, o_ref,
                oacc, qrot, kbwd, kv_scr, rprev, rstate, *, nseg, seg_len):
    h = pl.program_id(1)
    s = pl.program_id(2)
    cs = RET_CHUNK
    nchunk = seg_len // cs
    ld_f = ld_ref[0, h]
    ld_b = ld_ref[1, h]
    ii = lax.broadcasted_iota(jnp.int32, (cs, 1), 0).astype(F32)
    jj = lax.broadcasted_iota(jnp.int32, (1, cs), 1).astype(F32)
    k_scale = RET_HEAD_DIM ** -0.5
    nt = (((1,), (1,)), ((), ()))
    tn = (((0,), (0,)), ((), ()))
    chunk_len = jnp.full((1, 1), cs, F32)

    @pl.when(s == 0)
    def _():
        rstate[...] = jnp.zeros_like(rstate)

    @pl.when(s < nseg)
    def _():
        base = s * seg_len
        diff = ii - jj
        dmat = k_scale * jnp.where(diff >= 0, jnp.exp(ld_f * jnp.maximum(diff, 0.0)),
                                   jnp.exp(ld_b * jnp.maximum(-diff, 0.0)))
        q_dec = jnp.exp(ld_f * (ii + 1.0))
        k_dec_f = k_scale * jnp.exp(ld_f * (cs - 1.0 - ii))
        k_dec_b = k_scale * jnp.exp(ld_b * ii)
        g_chunk = jnp.exp(chunk_len * ld_f)

        def products(c, carry):
            r0 = pl.multiple_of(c * cs, cs)
            rows = pl.ds(pl.multiple_of(base + r0, cs), cs)
            cos = cos_ref[pl.ds(r0, cs), :]
            sin = sin_ref[pl.ds(r0, cs), :]
            q = _rotate(q_ref[0, pl.ds(r0, cs), :].astype(F32), cos, sin).astype(BF16)
            k = _rotate(k_ref[0, pl.ds(r0, cs), :].astype(F32), cos, sin)
            v = v_ref[0, pl.ds(r0, cs), :]
            qrot[rows, :] = q
            kbwd[rows, :] = (k * k_dec_b).astype(BF16)
            a = lax.dot_general(q, k.astype(BF16), nt, preferred_element_type=F32) * dmat
            oacc[rows, :] = jnp.dot(a.astype(BF16), v, preferred_element_type=F32)
            kv_scr[c] = lax.dot_general((k * k_dec_f).astype(BF16), v, tn, preferred_element_type=F32)
            return carry

        for c_static in range(nchunk):
            products(c_static, 0)

        def scan(c, carry):
            r = rstate[0]
            rprev[c] = r.astype(BF16)
            rstate[0] = g_chunk * r + kv_scr[c]
            return carry

        for c_static in range(nchunk):
            scan(c_static, 0)

        def cross(c, carry):
            rows = pl.ds(pl.multiple_of(base + c * cs, cs), cs)
            oacc[rows, :] += q_dec * jnp.dot(qrot[rows, :], rprev[c], preferred_element_type=F32)
            return carry

        for c_static in range(nchunk):
            cross(c_static, 0)

    @pl.when(s >= nseg)
    def _():
        base = (2 * nseg - 1 - s) * seg_len
        q_dec = jnp.exp(ld_b * (cs - ii))
        g_chunk = jnp.exp(chunk_len * ld_b)

        def products(c, carry):
            r0 = pl.multiple_of(c * cs, cs)
            rows = pl.ds(pl.multiple_of(base + r0, cs), cs)
            kv_scr[c] = lax.dot_general(kbwd[rows, :], v_ref[0, pl.ds(r0, cs), :], tn,
                                        preferred_element_type=F32)
            return carry

        for c_static in range(nchunk):
            products(c_static, 0)

        def scan(t, carry):
            c = nchunk - 1 - t
            r = rstate[1]
            rprev[c] = r.astype(BF16)
            rstate[1] = g_chunk * r + kv_scr[c]
            return carry

        for c_static in range(nchunk):
            scan(c_static, 0)

        def finish(c, carry):
            r0 = pl.multiple_of(c * cs, cs)
            rows = pl.ds(pl.multiple_of(base + r0, cs), cs)
            o = oacc[rows, :] + q_dec * jnp.dot(qrot[rows, :], rprev[c], preferred_element_type=F32)
            mu = jnp.mean(o, axis=-1, keepdims=True)
            oc = o - mu
            var = jnp.mean(oc * oc, axis=-1, keepdims=True)
            gate = g_ref[0, pl.ds(r0, cs), :].astype(F32)
            silu = gate * (1.0 / (1.0 + jnp.exp(-gate)))
            o_ref[0, pl.ds(r0, cs), :] = (oc * lax.rsqrt(var + EPS) * silu).astype(o_ref.dtype)
            return carry

        for c_static in range(nchunk):
            finish(c_static, 0)


def _rotary_tables(seq_len):
    half = RET_HEAD_DIM // 2
    inv = 1.0 / (10000.0 ** jnp.linspace(0.0, 1.0, half, dtype=F32))
    ang = jnp.arange(seq_len, dtype=F32)[:, None] * inv[None, :]
    return jnp.cos(ang), jnp.sin(ang)


def retention(proj, log_decay, cos, sin):
    b, s, _ = proj.shape
    seg_len = min(RET_SEG, s)
    nseg = s // seg_len
    nchunk = seg_len // RET_CHUNK
    base = (ATT_WIDTH + 2 * KV_WIDTH) // RET_HEAD_DIM

    def seg_both(t):
        return jnp.where(t < nseg, t, 2 * nseg - 1 - t)

    def seg_fwd_only(t):
        return jnp.minimum(t, nseg - 1)

    def seg_bwd_only(t):
        return jnp.where(t < nseg, nseg - 1, 2 * nseg - 1 - t)

    def head_block(which, seg):
        off = base + which * N_RET_HEADS
        return pl.BlockSpec((1, seg_len, RET_HEAD_DIM), lambda bi, h, t: (bi, seg(t), off + h))

    table = pl.BlockSpec((seg_len, RET_HEAD_DIM // 2), lambda bi, h, t: (seg_fwd_only(t), 0))
    return pl.pallas_call(
        functools.partial(_ret_kernel, nseg=nseg, seg_len=seg_len),
        grid=(b, N_RET_HEADS, 2 * nseg),
        in_specs=[pl.BlockSpec(memory_space=pltpu.SMEM),
                  head_block(0, seg_fwd_only), head_block(1, seg_fwd_only),
                  head_block(2, seg_both), head_block(3, seg_bwd_only), table, table],
        out_specs=pl.BlockSpec((1, seg_len, RET_HEAD_DIM),
                               lambda bi, h, t: (bi, seg_bwd_only(t), h)),
        out_shape=jax.ShapeDtypeStruct((b, s, RET_WIDTH), BF16),
        scratch_shapes=[pltpu.VMEM((s, RET_HEAD_DIM), F32),
                        pltpu.VMEM((s, RET_HEAD_DIM), BF16),
                        pltpu.VMEM((s, RET_HEAD_DIM), BF16),
                        pltpu.VMEM((nchunk, RET_HEAD_DIM, RET_HEAD_DIM), F32),
                        pltpu.VMEM((nchunk, RET_HEAD_DIM, RET_HEAD_DIM), BF16),
                        pltpu.VMEM((2, RET_HEAD_DIM, RET_HEAD_DIM), F32)],
        compiler_params=_params("parallel", "arbitrary", "arbitrary"),
        name="retention",
    )(log_decay, proj, proj, proj, proj, cos, sin)


def _outproj_kernel(att_ref, ret_ref, x_ref, wa_ref, wr_ref, g_ref, wrt_ref, xo_ref, hn_ref, aff_ref, ybuf):
    i = pl.program_id(0)
    cur = lax.rem(i, 2)

    @pl.when(i == 0)
    def _():
        ybuf[1] = jnp.zeros(ybuf.shape[1:], F32)

    y_prev = ybuf[1 - cur]
    var = jnp.mean(y_prev * y_prev, axis=-1, keepdims=True)
    hn = (y_prev * lax.rsqrt(var + EPS) * g_ref[...]).astype(BF16)
    hn_ref[...] = hn
    logits = jnp.dot(hn, wrt_ref[...], preferred_element_type=F32)
    lane = lax.broadcasted_iota(jnp.int32, (1, LANES), 1)
    logits = jnp.where(lane < N_EXPERTS, logits, NEG_INF)
    e = jnp.exp(logits - jnp.max(logits, axis=-1, keepdims=True))
    aff = e / jnp.sum(e, axis=-1, keepdims=True)
    aff_ref[...] = jnp.transpose(aff)[:N_EXPERTS]

    y = x_ref[...] + jnp.dot(att_ref[...], wa_ref[...], preferred_element_type=F32)
    y = y + jnp.dot(ret_ref[...], wr_ref[...], preferred_element_type=F32)
    xo_ref[...] = y
    ybuf[cur] = y


def outproj_norm_router(att, ret, x2d, w_out_bf16, g, w_router_pad):
    n, d = x2d.shape
    tm = min(ROW_TILE_OUTPROJ, n)
    nt = n // tm
    row = lambda i: (jnp.minimum(i, nt - 1), 0)
    prev_row = lambda i: (jnp.maximum(i - 1, 0), 0)
    return pl.pallas_call(
        _outproj_kernel,
        grid=(nt + 1,),
        in_specs=[
            pl.BlockSpec((tm, ATT_WIDTH), row),
            pl.BlockSpec((tm, RET_WIDTH), row),
            pl.BlockSpec((tm, d), row),
            pl.BlockSpec((ATT_WIDTH, d), lambda i: (0, 0)),
            pl.BlockSpec((RET_WIDTH, d), lambda i: (1, 0)),
            pl.BlockSpec((1, d), lambda i: (0, 0)),
            pl.BlockSpec((d, LANES), lambda i: (0, 0)),
        ],
        out_specs=[pl.BlockSpec((tm, d), row), pl.BlockSpec((tm, d), prev_row),
                   pl.BlockSpec((N_EXPERTS, tm), lambda i: (0, jnp.maximum(i - 1, 0)))],
        out_shape=[jax.ShapeDtypeStruct((n, d), F32), jax.ShapeDtypeStruct((n, d), BF16),
                   jax.ShapeDtypeStruct((N_EXPERTS, n), F32)],
        scratch_shapes=[pltpu.VMEM((2, tm, d), F32)],
        compiler_params=_params("arbitrary"),
        name="outproj_norm_router",
    )(att, ret, x2d, w_out_bf16, w_out_bf16, g.reshape(1, d), w_router_pad)


def _moe_kernel(xe_ref, wg_ref, wu_ref, wd_ref, gate_ref, o_ref):
    x = xe_ref[0]
    hg = jnp.dot(x, wg_ref[0], preferred_element_type=F32)
    hu = jnp.dot(x, wu_ref[0], preferred_element_type=F32)
    h = hg * (1.0 / (1.0 + jnp.exp(-hg))) * hu
    y = jnp.dot(h.astype(BF16), wd_ref[0], preferred_element_type=F32)
    o_ref[0] = (y * gate_ref[0]).astype(o_ref.dtype)


def expert_ffn(xe, gates, wg, wu, wd):
    e, cap, d = xe.shape
    ff = wg.shape[-1]
    tc = min(MOE_SLOT_TILE, cap)
    return pl.pallas_call(
        _moe_kernel,
        grid=(e, cap // tc),
        in_specs=[
            pl.BlockSpec((1, tc, d), lambda ei, t: (ei, t, 0)),
            pl.BlockSpec((1, d, ff), lambda ei, t: (ei, 0, 0)),
            pl.BlockSpec((1, d, ff), lambda ei, t: (ei, 0, 0)),
            pl.BlockSpec((1, ff, d), lambda ei, t: (ei, 0, 0)),
            pl.BlockSpec((1, tc, 1), lambda ei, t: (ei, t, 0)),
        ],
        out_specs=pl.BlockSpec((1, tc, d), lambda ei, t: (ei, t, 0)),
        out_shape=jax.ShapeDtypeStruct((e, cap, d), BF16),
        compiler_params=_params("parallel", "arbitrary"),
        name="expert_ffn",
    )(xe, wg, wu, wd, gates)


def _combine_kernel(lo_ref, hi_ref, x_ref, tok_ref, g_ref, ye_hbm, o_ref, buf, sem, *, tile, blk, apply_norm):
    i = pl.program_id(0)
    lo = lo_ref[i]
    hi = hi_ref[i]
    t0 = i * tile
    ahead = COMBINE_BUFFERS - 1

    def fetch(c):
        slot = lax.rem(c, COMBINE_BUFFERS)
        return pltpu.make_async_copy(ye_hbm.at[pl.ds(pl.multiple_of(c * blk, blk), blk)],
                                     buf.at[slot], sem.at[slot])

    def start_first_blocks(first, end):
        for k in range(ahead):
            @pl.when(first + k < end)
            def _():
                fetch(first + k).start()

    @pl.when(i == 0)
    def _():
        start_first_blocks(lo, hi)

    o_ref[...] = x_ref[...]
    row = lax.broadcasted_iota(jnp.int32, (tile, blk), 0)

    def body(c, carry):
        fetch(c).wait()

        @pl.when(c + ahead < hi)
        def _():
            fetch(c + ahead).start()

        rel = tok_ref[pl.ds(c, 1), :] - t0
        sel = jnp.where(row == rel, 1.0, 0.0).astype(BF16)
        o_ref[...] += jnp.dot(sel, buf[lax.rem(c, COMBINE_BUFFERS)], preferred_element_type=F32)
        return carry

    lax.fori_loop(lo, hi, body, 0)

    @pl.when(i + 1 < pl.num_programs(0))
    def _():
        start_first_blocks(lo_ref[i + 1], hi_ref[i + 1])

    if apply_norm:
        y = o_ref[...]
        var = jnp.mean(y * y, axis=-1, keepdims=True)
        o_ref[...] = y * lax.rsqrt(var + EPS) * g_ref[...]


def combine(x2d, ye_sorted, tok_sorted, final_g):
    n, d = x2d.shape
    rows = ye_sorted.shape[0]
    tile = min(COMBINE_TOKEN_TILE, n)
    blk = min(COMBINE_ROW_BLOCK, rows)
    bounds = jnp.searchsorted(tok_sorted, jnp.arange(0, n + 1, tile, dtype=jnp.int32)).astype(jnp.int32)
    start, end = bounds[:-1], bounds[1:]
    lo = start // blk
    hi = jnp.where(end > start, (end + blk - 1) // blk, lo)
    apply_norm = final_g is not None
    g = (final_g if apply_norm else jnp.ones((d,), F32)).reshape(1, d)
    return pl.pallas_call(
        functools.partial(_combine_kernel, tile=tile, blk=blk, apply_norm=apply_norm),
        grid=(n // tile,),
        in_specs=[pl.BlockSpec(memory_space=pltpu.SMEM), pl.BlockSpec(memory_space=pltpu.SMEM),
                  pl.BlockSpec((tile, d), lambda i: (i, 0)),
                  pl.BlockSpec((rows // blk, blk), lambda i: (0, 0)),
                  pl.BlockSpec((1, d), lambda i: (0, 0)),
                  pl.BlockSpec(memory_space=pl.ANY)],
        out_specs=pl.BlockSpec((tile, d), lambda i: (i, 0)),
        out_shape=jax.ShapeDtypeStruct((n, d), F32),
        scratch_shapes=[pltpu.VMEM((COMBINE_BUFFERS, blk, d), BF16),
                        pltpu.SemaphoreType.DMA((COMBINE_BUFFERS,))],
        compiler_params=_params("arbitrary"),
        name="combine",
    )(lo, hi, x2d, tok_sorted.reshape(rows // blk, blk), g, ye_sorted)


def _layer(x2d, bsz, seq, lw, moe_stacks, moe_cache, band_bias, cos, sin, final_g):
    n = bsz * seq
    layer = lw["layer"]
    if layer in moe_cache:
        proj, _ = norm_inproj(x2d, lw["norm_mix_g"], lw["w_in"])
    elif all(side_cast_fits(n, IN_WIDTH, st) for st in moe_stacks):
        proj, moe_cache[layer] = norm_inproj(x2d, lw["norm_mix_g"], lw["w_in"], moe_stacks, layer)
    else:
        proj, _ = norm_inproj(x2d, lw["norm_mix_g"], lw["w_in"])
        moe_cache[layer] = [layer_weight_bf16(st, layer) for st in moe_stacks]
    w_gate, w_up, w_down = moe_cache[layer]
    proj = proj.reshape(bsz, seq, IN_WIDTH)
    att = window_attention(proj, band_bias, lw["attn_sink"]).reshape(n, ATT_WIDTH)
    ret = retention(proj, lw["log_decay"], cos, sin).reshape(n, RET_WIDTH)
    x2d, hn, aff = outproj_norm_router(att, ret, x2d, lw["w_out"], lw["norm_ffn_g"], lw["w_router"])
    cap = CAPACITY_FACTOR * n // N_EXPERTS
    gates, idx = lax.top_k(aff, cap)
    ye = expert_ffn(hn[idx], gates[..., None], w_gate, w_up, w_down)
    slots = N_EXPERTS * cap
    tok_sorted, perm = lax.sort_key_val(idx.reshape(slots).astype(jnp.int32),
                                        jnp.arange(slots, dtype=jnp.int32))
    return combine(x2d, ye.reshape(slots, D_MODEL)[perm], tok_sorted, final_g)


def kernel(x_prompt, x_sample, norm_mix_g, w_in, attn_sink, rel_bias, ret_decay_fwd, ret_decay_bwd,
           w_out, norm_ffn_g, w_router, w_gate, w_up, w_down, norm_final_g):
    depth = w_in.shape[0]
    band_bias = _band_bias(rel_bias)
    layers = []
    for l in range(depth):
        layers.append({
            "layer": l,
            "norm_mix_g": norm_mix_g[l],
            "w_in": layer_weight_bf16(w_in, l),
            "attn_sink": attn_sink[l],
            "log_decay": jnp.stack([jax.nn.log_sigmoid(ret_decay_fwd[l].astype(F32)),
                                    jax.nn.log_sigmoid(ret_decay_bwd[l].astype(F32))]),
            "w_out": layer_weight_bf16(w_out, l),
            "norm_ffn_g": norm_ffn_g[l],
            "w_router": jnp.pad(w_router[l].astype(BF16), ((0, 0), (0, LANES - N_EXPERTS))),
        })
    moe_stacks = (w_gate, w_up, w_down)
    moe_cache = {}

    def trunk(x):
        bsz, seq, d = x.shape
        cos, sin = _rotary_tables(seq)
        x2d = x.reshape(bsz * seq, d)
        for l, lw in enumerate(layers):
            x2d = _layer(x2d, bsz, seq, lw, moe_stacks, moe_cache, band_bias, cos, sin,
                         norm_final_g if l == depth - 1 else None)
        return x2d.reshape(bsz, seq, d)

    return (trunk(x_prompt), trunk(x_sample))
```

```python
import functools
import math

import jax
import jax.numpy as jnp
import numpy as np
from jax import lax
from jax.experimental import pallas as pl
from jax.experimental.pallas import tpu as pltpu

D_MODEL = 2048
HEAD_DIM_ATT = 128
N_Q_HEADS = 8
N_KV_HEADS = 2
GQA = N_Q_HEADS // N_KV_HEADS
ATT_WIDTH = N_Q_HEADS * HEAD_DIM_ATT
KV_WIDTH = N_KV_HEADS * HEAD_DIM_ATT
WINDOW = 128
BLOCK = 128
N_BUCKETS = 32
MAX_DISTANCE = 128
N_RET_HEADS = 4
RET_HEAD_DIM = 256
RET_WIDTH = N_RET_HEADS * RET_HEAD_DIM
RET_CHUNK = 256
D_MIX = ATT_WIDTH + RET_WIDTH
IN_WIDTH = ATT_WIDTH + 2 * KV_WIDTH + 4 * RET_WIDTH
N_EXPERTS = 16
CAPACITY_FACTOR = 2
EXPERT_FF = D_MODEL // 2
EPS = 1e-6
NEG_INF = -1e30
LANES = 128
LOG2_E = math.log2(math.e)
SINK_ROWS = 16

VMEM_LIMIT_BYTES = 56 * 1024 * 1024

ROW_TILE_INPROJ = 512
COL_TILE_INPROJ = 2816
ROW_TILE_OUTPROJ = 512
ATT_Q_TILE = 512
RET_SEG = 2048
MOE_SLOT_TILE = 512
COMBINE_TOKEN_TILE = 256
CAST_BLOCK_ELEMS = 1024 * 1024
SIDE_CAST_BLOCK_ELEMS = 512 * 1024
COMBINE_ROW_BLOCK = 256
COMBINE_BUFFERS = 4

BF16 = jnp.bfloat16
F32 = jnp.float32


def _params(*sem):
    return pltpu.CompilerParams(dimension_semantics=sem, vmem_limit_bytes=VMEM_LIMIT_BYTES)


def _cast_kernel(w_ref, o_ref):
    o_ref[...] = w_ref[0].astype(o_ref.dtype)


def layer_weight_bf16(w_stacked, layer):
    shape = w_stacked.shape[1:]
    cols = shape[-1]
    rows = math.prod(shape[:-1])
    tr = rows
    while tr * cols > CAST_BLOCK_ELEMS and tr % 32 == 0:
        tr //= 2
    out = pl.pallas_call(
        _cast_kernel,
        grid=(rows // tr,),
        in_specs=[pl.BlockSpec((1, tr, cols), lambda i: (layer, i, 0))],
        out_specs=pl.BlockSpec((tr, cols), lambda i: (i, 0)),
        out_shape=jax.ShapeDtypeStruct((rows, cols), BF16),
        compiler_params=_params("parallel"),
        name="weight_cast",
    )(w_stacked.reshape(w_stacked.shape[0], rows, cols))
    return out.reshape(shape)


def _norm_inproj_kernel(x_ref, g_ref, w_ref, *rest, n_cast):
    cast_in = rest[:n_cast]
    o_ref = rest[n_cast]
    cast_out = rest[n_cast + 1:2 * n_cast + 1]
    h_ref = rest[2 * n_cast + 1]

    @pl.when(pl.program_id(1) == 0)
    def _():
        x = x_ref[...]
        var = jnp.mean(x * x, axis=-1, keepdims=True)
        h_ref[...] = (x * lax.rsqrt(var + EPS) * g_ref[...]).astype(BF16)

    o_ref[...] = jnp.dot(h_ref[...], w_ref[...], preferred_element_type=F32).astype(o_ref.dtype)
    for src, dst in zip(cast_in, cast_out):
        dst[...] = src[0].astype(dst.dtype)


def side_cast_fits(n, width, stacked):
    steps = (n // min(ROW_TILE_INPROJ, n)) * (width // COL_TILE_INPROJ)
    rows = math.prod(stacked.shape[1:-1])
    per_step = rows // steps
    return (rows % steps == 0 and per_step % 16 == 0
            and per_step * stacked.shape[-1] <= SIDE_CAST_BLOCK_ELEMS)


def norm_inproj(x2d, g, w_bf16, cast_stacks=(), cast_layer=0):
    n, d = x2d.shape
    width = w_bf16.shape[1]
    tm = min(ROW_TILE_INPROJ, n)
    tn = COL_TILE_INPROJ
    ncol = width // tn
    steps = (n // tm) * ncol
    in_specs = [
        pl.BlockSpec((tm, d), lambda i, j: (i, 0)),
        pl.BlockSpec((1, d), lambda i, j: (0, 0)),
        pl.BlockSpec((d, tn), lambda i, j: (0, j)),
    ]
    out_specs = [pl.BlockSpec((tm, tn), lambda i, j: (i, j))]
    out_shape = [jax.ShapeDtypeStruct((n, width), BF16)]
    cast_args = []
    for st in cast_stacks:
        cols = st.shape[-1]
        rows = math.prod(st.shape[1:-1])
        per_step = rows // steps
        in_specs.append(pl.BlockSpec((1, per_step, cols), lambda i, j: (cast_layer, i * ncol + j, 0)))
        out_specs.append(pl.BlockSpec((per_step, cols), lambda i, j: (i * ncol + j, 0)))
        out_shape.append(jax.ShapeDtypeStruct((rows, cols), BF16))
        cast_args.append(st.reshape(st.shape[0], rows, cols))
    outs = pl.pallas_call(
        functools.partial(_norm_inproj_kernel, n_cast=len(cast_stacks)),
        grid=(n // tm, ncol),
        in_specs=in_specs,
        out_specs=out_specs,
        out_shape=out_shape,
        scratch_shapes=[pltpu.VMEM((tm, d), BF16)],
        compiler_params=_params("parallel", "arbitrary"),
        name="norm_inproj",
    )(x2d, g.reshape(1, d), w_bf16, *cast_args)
    return outs[0], [o.reshape(st.shape[1:]) for o, st in zip(outs[1:], cast_stacks)]


def _t5_bucket(rel):
    half = N_BUCKETS // 2
    max_exact = half // 2
    bucket = jnp.where(rel > 0, half, 0)
    n = jnp.abs(rel)
    nf = jnp.maximum(n, 1).astype(F32)
    large = max_exact + (jnp.log(nf / max_exact) / math.log(MAX_DISTANCE / max_exact)
                         * (half - max_exact)).astype(jnp.int32)
    large = jnp.minimum(large, half - 1)
    return bucket + jnp.where(n < max_exact, n, large)


def _band_bias(rel_bias):
    qi = jnp.arange(BLOCK)[:, None]
    kj = jnp.arange(3 * BLOCK)[None, :]
    rel = kj - BLOCK - qi
    onehot = jax.nn.one_hot(_t5_bucket(rel), N_BUCKETS, dtype=F32)
    bias = jnp.einsum('qkb,bh->qkh', onehot, rel_bias.astype(F32),
                      precision=lax.Precision.HIGHEST)
    bias = jnp.where((jnp.abs(rel) <= WINDOW)[:, :, None], bias, NEG_INF)
    bias = bias.reshape(BLOCK, 3, BLOCK, N_KV_HEADS, GQA).transpose(3, 1, 2, 4, 0)
    bias = bias.reshape(N_KV_HEADS, 3, BLOCK, GQA * BLOCK)
    return jnp.concatenate([bias, jnp.full_like(bias[:, :1], NEG_INF)], axis=1) * LOG2_E


def _attn_kernel(q_ref, kp_ref, km_ref, kn_ref, vp_ref, vm_ref, vn_ref, bias_ref, sink_ref,
                 o_ref, k_scr, v_scr, *, tq):
    i = pl.program_id(1)
    last = pl.num_programs(1) - 1
    k_scr[0:BLOCK] = kp_ref[0]
    k_scr[BLOCK:BLOCK + tq] = km_ref[0]
    k_scr[BLOCK + tq:] = kn_ref[0]
    v_scr[0:BLOCK] = vp_ref[0]
    v_scr[BLOCK:BLOCK + tq] = vm_ref[0]
    v_scr[BLOCK + tq:] = vn_ref[0]
    scale = HEAD_DIM_ATT ** -0.5 * LOG2_E
    nsb = tq // BLOCK
    nt = (((1,), (1,)), ((), ()))
    tn = (((0,), (0,)), ((), ()))
    ones = jnp.ones((3 * BLOCK, HEAD_DIM_ATT), BF16)
    r_id = lax.broadcasted_iota(jnp.int32, (SINK_ROWS, 2 * HEAD_DIM_ATT), 0)
    c_id = lax.broadcasted_iota(jnp.int32, (SINK_ROWS, 2 * HEAD_DIM_ATT), 1)
    sink_rows = jnp.where((r_id == 0) & (c_id >= HEAD_DIM_ATT), 1.0, 0.0).astype(BF16)
    for sb in range(nsb):
        for h in range(N_KV_HEADS):
            c0 = h * HEAD_DIM_ATT
            qs = jnp.concatenate(
                [q_ref[0, sb * BLOCK:(sb + 1) * BLOCK,
                       (h * GQA + g) * HEAD_DIM_ATT:(h * GQA + g + 1) * HEAD_DIM_ATT]
                 for g in range(GQA)], axis=0)
            parts = []
            for c in range(3):
                kc = k_scr[(sb + c) * BLOCK:(sb + c + 1) * BLOCK, c0:c0 + HEAD_DIM_ATT]
                if c == 0 and sb == 0:
                    slot = jnp.where(i == 0, 3, 0)
                elif c == 2 and sb == nsb - 1:
                    slot = jnp.where(i == last, 3, 2)
                else:
                    slot = c
                s_c = lax.dot_general(kc, qs, nt, preferred_element_type=F32)
                parts.append(s_c * scale + bias_ref[h, slot])
            s = jnp.concatenate(parts, axis=0)
            sink = sink_ref[h:h + 1, :]
            m = jnp.maximum(jnp.max(s, axis=0, keepdims=True), sink)
            p = jnp.exp2(s - m).astype(BF16)
            p_sink = jnp.broadcast_to(jnp.exp2(sink - m).astype(BF16), (SINK_ROWS, GQA * BLOCK))
            p_ext = jnp.concatenate([p, p_sink], axis=0)
            vb = v_scr[sb * BLOCK:(sb + 3) * BLOCK, c0:c0 + HEAD_DIM_ATT]
            v_ext = jnp.concatenate([jnp.concatenate([vb, ones], axis=1), sink_rows], axis=0)
            o_ext = lax.dot_general(p_ext, v_ext, tn, preferred_element_type=F32)
            o = o_ext[:, :HEAD_DIM_ATT] / o_ext[:, HEAD_DIM_ATT:]
            for g in range(GQA):
                o_ref[0, sb * BLOCK:(sb + 1) * BLOCK,
                      (h * GQA + g) * HEAD_DIM_ATT:(h * GQA + g + 1) * HEAD_DIM_ATT] = (
                    o[g * BLOCK:(g + 1) * BLOCK].astype(o_ref.dtype))


def window_attention(proj, band_bias, sink):
    b, s, _ = proj.shape
    tq = min(ATT_Q_TILE, s)
    nblk = s // BLOCK
    r = tq // BLOCK
    assert r >= 2 and s % tq == 0
    k_col = ATT_WIDTH // KV_WIDTH
    v_col = k_col + 1

    def main(col):
        return pl.BlockSpec((1, tq, KV_WIDTH), lambda bi, i: (bi, i, col))

    def prev(col):
        return pl.BlockSpec((1, BLOCK, KV_WIDTH), lambda bi, i: (bi, jnp.maximum(i * r - 1, 0), col))

    def nxt(col):
        return pl.BlockSpec((1, BLOCK, KV_WIDTH),
                            lambda bi, i: (bi, jnp.minimum((i + 1) * r, nblk - 1), col))

    sink_rows = jnp.repeat(sink.astype(F32) * LOG2_E, BLOCK).reshape(N_KV_HEADS, GQA * BLOCK)
    return pl.pallas_call(
        functools.partial(_attn_kernel, tq=tq),
        grid=(b, s // tq),
        in_specs=[
            pl.BlockSpec((1, tq, ATT_WIDTH), lambda bi, i: (bi, i, 0)),
            prev(k_col), main(k_col), nxt(k_col),
            prev(v_col), main(v_col), nxt(v_col),
            pl.BlockSpec((N_KV_HEADS, 4, BLOCK, GQA * BLOCK), lambda bi, i: (0, 0, 0, 0)),
            pl.BlockSpec((N_KV_HEADS, GQA * BLOCK), lambda bi, i: (0, 0)),
        ],
        out_specs=pl.BlockSpec((1, tq, ATT_WIDTH), lambda bi, i: (bi, i, 0)),
        out_shape=jax.ShapeDtypeStruct((b, s, ATT_WIDTH), BF16),
        scratch_shapes=[pltpu.VMEM((tq + 2 * BLOCK, KV_WIDTH), BF16),
                        pltpu.VMEM((tq + 2 * BLOCK, KV_WIDTH), BF16)],
        compiler_params=_params("parallel", "arbitrary"),
        name="window_attention",
    )(proj, proj, proj, proj, proj, proj, proj, band_bias, sink_rows)


def _rotate(x, cos, sin):
    half = RET_HEAD_DIM // 2
    x1 = x[:, :half]
    x2 = x[:, half:]
    return jnp.concatenate([x1 * cos - x2 * sin, x1 * sin + x2 * cos], axis=-1)


def _ret_kernel(ld_ref, q_ref, k_ref, v_ref, g_ref, cos_ref, sin_ref, o_ref,
                oacc, qrot, kbwd, kv_scr, rprev, rstate, *, nseg, seg_len):
    h = pl.program_id(1)
    s = pl.program_id(2)
    cs = RET_CHUNK
    nchunk = seg_len // cs
    ld_f = ld_ref[0, h]
    ld_b = ld_ref[1, h]
    ii = lax.broadcasted_iota(jnp.int32, (cs, 1), 0).astype(F32)
    jj = lax.broadcasted_iota(jnp.int32, (1, cs), 1).astype(F32)
    k_scale = RET_HEAD_DIM ** -0.5
    nt = (((1,), (1,)), ((), ()))
    tn = (((0,), (0,)), ((), ()))
    chunk_len = jnp.full((1, 1), cs, F32)

    @pl.when(s == 0)
    def _():
        rstate[...] = jnp.zeros_like(rstate)

    @pl.when(s < nseg)
    def _():
        base = s * seg_len
        diff = ii - jj
        dmat = k_scale * jnp.where(diff >= 0, jnp.exp(ld_f * jnp.maximum(diff, 0.0)),
                                   jnp.exp(ld_b * jnp.maximum(-diff, 0.0)))
        q_dec = jnp.exp(ld_f * (ii + 1.0))
        k_dec_f = k_scale * jnp.exp(ld_f * (cs - 1.0 - ii))
        k_dec_b = k_scale * jnp.exp(ld_b * ii)
        g_chunk = jnp.exp(chunk_len * ld_f)

        def products(c, carry):
            r0 = pl.multiple_of(c * cs, cs)
            rows = pl.ds(pl.multiple_of(base + r0, cs), cs)
            cos = cos_ref[pl.ds(r0, cs), :]
            sin = sin_ref[pl.ds(r0, cs), :]
            q = _rotate(q_ref[0, pl.ds(r0, cs), :].astype(F32), cos, sin).astype(BF16)
            k = _rotate(k_ref[0, pl.ds(r0, cs), :].astype(F32), cos, sin)
            v = v_ref[0, pl.ds(r0, cs), :]
            qrot[rows, :] = q
            kbwd[rows, :] = (k * k_dec_b).astype(BF16)
            a = lax.dot_general(q, k.astype(BF16), nt, preferred_element_type=F32) * dmat
            oacc[rows, :] = jnp.dot(a.astype(BF16), v, preferred_element_type=F32)
            kv_scr[c] = lax.dot_general((k * k_dec_f).astype(BF16), v, tn, preferred_element_type=F32)
            return carry

        for c_static in range(nchunk):
            products(c_static, 0)

        def scan(c, carry):
            r = rstate[0]
            rprev[c] = r.astype(BF16)
            rstate[0] = g_chunk * r + kv_scr[c]
            return carry

        for c_static in range(nchunk):
            scan(c_static, 0)

        def cross(c, carry):
            rows = pl.ds(pl.multiple_of(base + c * cs, cs), cs)
            oacc[rows, :] += q_dec * jnp.dot(qrot[rows, :], rprev[c], preferred_element_type=F32)
            return carry

        for c_static in range(nchunk):
            cross(c_static, 0)

    @pl.when(s >= nseg)
    def _():
        base = (2 * nseg - 1 - s) * seg_len
        q_dec = jnp.exp(ld_b * (cs - ii))
        g_chunk = jnp.exp(chunk_len * ld_b)

        def products(c, carry):
            r0 = pl.multiple_of(c * cs, cs)
            rows = pl.ds(pl.multiple_of(base + r0, cs), cs)
            kv_scr[c] = lax.dot_general(kbwd[rows, :], v_ref[0, pl.ds(r0, cs), :], tn,
                                        preferred_element_type=F32)
            return carry

        for c_static in range(nchunk):
            products(c_static, 0)

        def scan(t, carry):
            c = nchunk - 1 - t
            r = rstate[1]
            rprev[c] = r.astype(BF16)
            rstate[1] = g_chunk * r + kv_scr[c]
            return carry

        for c_static in range(nchunk):
            scan(c_static, 0)

        def finish(c, carry):
            r0 = pl.multiple_of(c * cs, cs)
            rows = pl.ds(pl.multiple_of(base + r0, cs), cs)
            o = oacc[rows, :] + q_dec * jnp.dot(qrot[rows, :], rprev[c], preferred_element_type=F32)
            mu = jnp.mean(o, axis=-1, keepdims=True)
            oc = o - mu
            var = jnp.mean(oc * oc, axis=-1, keepdims=True)
            gate = g_ref[0, pl.ds(r0, cs), :].astype(F32)
            silu = gate * (1.0 / (1.0 + jnp.exp(-gate)))
            o_ref[0, pl.ds(r0, cs), :] = (oc * lax.rsqrt(var + EPS) * silu).astype(o_ref.dtype)
            return carry

        for c_static in range(nchunk):
            finish(c_static, 0)


def _rotary_tables(seq_len):
    half = RET_HEAD_DIM // 2
    inv = 1.0 / (10000.0 ** jnp.linspace(0.0, 1.0, half, dtype=F32))
    ang = jnp.arange(seq_len, dtype=F32)[:, None] * inv[None, :]
    return jnp.cos(ang), jnp.sin(ang)


def retention(proj, log_decay, cos, sin):
    b, s, _ = proj.shape
    seg_len = min(RET_SEG, s)
    nseg = s // seg_len
    nchunk = seg_len // RET_CHUNK
    base = (ATT_WIDTH + 2 * KV_WIDTH) // RET_HEAD_DIM

    def seg_both(t):
        return jnp.where(t < nseg, t, 2 * nseg - 1 - t)

    def seg_fwd_only(t):
        return jnp.minimum(t, nseg - 1)

    def seg_bwd_only(t):
        return jnp.where(t < nseg, nseg - 1, 2 * nseg - 1 - t)

    def head_block(which, seg):
        off = base + which * N_RET_HEADS
        return pl.BlockSpec((1, seg_len, RET_HEAD_DIM), lambda bi, h, t: (bi, seg(t), off + h))

    table = pl.BlockSpec((seg_len, RET_HEAD_DIM // 2), lambda bi, h, t: (seg_fwd_only(t), 0))
    return pl.pallas_call(
        functools.partial(_ret_kernel, nseg=nseg, seg_len=seg_len),
        grid=(b, N_RET_HEADS, 2 * nseg),
        in_specs=[pl.BlockSpec(memory_space=pltpu.SMEM),
                  head_block(0, seg_fwd_only), head_block(1, seg_fwd_only),
                  head_block(2, seg_both), head_block(3, seg_bwd_only), table, table],
        out_specs=pl.BlockSpec((1, seg_len, RET_HEAD_DIM),
                               lambda bi, h, t: (bi, seg_bwd_only(t), h)),
        out_shape=jax.ShapeDtypeStruct((b, s, RET_WIDTH), BF16),
        scratch_shapes=[pltpu.VMEM((s, RET_HEAD_DIM), F32),
                        pltpu.VMEM((s, RET_HEAD_DIM), BF16),
                        pltpu.VMEM((s, RET_HEAD_DIM), BF16),
                        pltpu.VMEM((nchunk, RET_HEAD_DIM, RET_HEAD_DIM), F32),
                        pltpu.VMEM((nchunk, RET_HEAD_DIM, RET_HEAD_DIM), BF16),
                        pltpu.VMEM((2, RET_HEAD_DIM, RET_HEAD_DIM), F32)],
        compiler_params=_params("parallel", "arbitrary", "arbitrary"),
        name="retention",
    )(log_decay, proj, proj, proj, proj, cos, sin)


def _outproj_kernel(att_ref, ret_ref, x_ref, wa_ref, wr_ref, g_ref, wrt_ref, xo_ref, hn_ref, aff_ref, ybuf):
    i = pl.program_id(0)
    cur = lax.rem(i, 2)

    @pl.when(i == 0)
    def _():
        ybuf[1] = jnp.zeros(ybuf.shape[1:], F32)

    y_prev = ybuf[1 - cur]
    var = jnp.mean(y_prev * y_prev, axis=-1, keepdims=True)
    hn = (y_prev * lax.rsqrt(var + EPS) * g_ref[...]).astype(BF16)
    hn_ref[...] = hn
    logits = jnp.dot(hn, wrt_ref[...], preferred_element_type=F32)
    lane = lax.broadcasted_iota(jnp.int32, (1, LANES), 1)
    logits = jnp.where(lane < N_EXPERTS, logits, NEG_INF)
    e = jnp.exp(logits - jnp.max(logits, axis=-1, keepdims=True))
    aff = e / jnp.sum(e, axis=-1, keepdims=True)
    aff_ref[...] = jnp.transpose(aff)[:N_EXPERTS]

    y = x_ref[...] + jnp.dot(att_ref[...], wa_ref[...], preferred_element_type=F32)
    y = y + jnp.dot(ret_ref[...], wr_ref[...], preferred_element_type=F32)
    xo_ref[...] = y
    ybuf[cur] = y


def outproj_norm_router(att, ret, x2d, w_out_bf16, g, w_router_pad):
    n, d = x2d.shape
    tm = min(ROW_TILE_OUTPROJ, n)
    nt = n // tm
    row = lambda i: (jnp.minimum(i, nt - 1), 0)
    prev_row = lambda i: (jnp.maximum(i - 1, 0), 0)
    return pl.pallas_call(
        _outproj_kernel,
        grid=(nt + 1,),
        in_specs=[
            pl.BlockSpec((tm, ATT_WIDTH), row),
            pl.BlockSpec((tm, RET_WIDTH), row),
            pl.BlockSpec((tm, d), row),
            pl.BlockSpec((ATT_WIDTH, d), lambda i: (0, 0)),
            pl.BlockSpec((RET_WIDTH, d), lambda i: (1, 0)),
            pl.BlockSpec((1, d), lambda i: (0, 0)),
            pl.BlockSpec((d, LANES), lambda i: (0, 0)),
        ],
        out_specs=[pl.BlockSpec((tm, d), row), pl.BlockSpec((tm, d), prev_row),
                   pl.BlockSpec((N_EXPERTS, tm), lambda i: (0, jnp.maximum(i - 1, 0)))],
        out_shape=[jax.ShapeDtypeStruct((n, d), F32), jax.ShapeDtypeStruct((n, d), BF16),
                   jax.ShapeDtypeStruct((N_EXPERTS, n), F32)],
        scratch_shapes=[pltpu.VMEM((2, tm, d), F32)],
        compiler_params=_params("arbitrary"),
        name="outproj_norm_router",
    )(att, ret, x2d, w_out_bf16, w_out_bf16, g.reshape(1, d), w_router_pad)


def _moe_kernel(xe_ref, wg_ref, wu_ref, wd_ref, gate_ref, o_ref):
    x = xe_ref[0]
    hg = jnp.dot(x, wg_ref[0], preferred_element_type=F32)
    hu = jnp.dot(x, wu_ref[0], preferred_element_type=F32)
    h = hg * (1.0 / (1.0 + jnp.exp(-hg))) * hu
    y = jnp.dot(h.astype(BF16), wd_ref[0], preferred_element_type=F32)
    o_ref[0] = (y * gate_ref[0]).astype(o_ref.dtype)


def expert_ffn(xe, gates, wg, wu, wd):
    e, cap, d = xe.shape
    ff = wg.shape[-1]
    tc = min(MOE_SLOT_TILE, cap)
    return pl.pallas_call(
        _moe_kernel,
        grid=(e, cap // tc),
        in_specs=[
            pl.BlockSpec((1, tc, d), lambda ei, t: (ei, t, 0)),
            pl.BlockSpec((1, d, ff), lambda ei, t: (ei, 0, 0)),
            pl.BlockSpec((1, d, ff), lambda ei, t: (ei, 0, 0)),
            pl.BlockSpec((1, ff, d), lambda ei, t: (ei, 0, 0)),
            pl.BlockSpec((1, tc, 1), lambda ei, t: (ei, t, 0)),
        ],
        out_specs=pl.BlockSpec((1, tc, d), lambda ei, t: (ei, t, 0)),
        out_shape=jax.ShapeDtypeStruct((e, cap, d), BF16),
        compiler_params=_params("parallel", "arbitrary"),
        name="expert_ffn",
    )(xe, wg, wu, wd, gates)


def _combine_kernel(lo_ref, hi_ref, x_ref, tok_ref, g_ref, ye_hbm, o_ref, buf, sem, *, tile, blk, apply_norm):
    i = pl.program_id(0)
    lo = lo_ref[i]
    hi = hi_ref[i]
    t0 = i * tile
    ahead = COMBINE_BUFFERS - 1

    def fetch(c):
        slot = lax.rem(c, COMBINE_BUFFERS)
        return pltpu.make_async_copy(ye_hbm.at[pl.ds(pl.multiple_of(c * blk, blk), blk)],
                                     buf.at[slot], sem.at[slot])

    def start_first_blocks(first, end):
        for k in range(ahead):
            @pl.when(first + k < end)
            def _():
                fetch(first + k).start()

    @pl.when(i == 0)
    def _():
        start_first_blocks(lo, hi)

    o_ref[...] = x_ref[...]
    row = lax.broadcasted_iota(jnp.int32, (tile, blk), 0)

    def body(c, carry):
        fetch(c).wait()

        @pl.when(c + ahead < hi)
        def _():
            fetch(c + ahead).start()

        rel = tok_ref[pl.ds(c, 1), :] - t0
        sel = jnp.where(row == rel, 1.0, 0.0).astype(BF16)
        o_ref[...] += jnp.dot(sel, buf[lax.rem(c, COMBINE_BUFFERS)], preferred_element_type=F32)
        return carry

    lax.fori_loop(lo, hi, body, 0)

    @pl.when(i + 1 < pl.num_programs(0))
    def _():
        start_first_blocks(lo_ref[i + 1], hi_ref[i + 1])

    if apply_norm:
        y = o_ref[...]
        var = jnp.mean(y * y, axis=-1, keepdims=True)
        o_ref[...] = y * lax.rsqrt(var + EPS) * g_ref[...]


def combine(x2d, ye_sorted, tok_sorted, final_g):
    n, d = x2d.shape
    rows = ye_sorted.shape[0]
    tile = min(COMBINE_TOKEN_TILE, n)
    blk = min(COMBINE_ROW_BLOCK, rows)
    bounds = jnp.searchsorted(tok_sorted, jnp.arange(0, n + 1, tile, dtype=jnp.int32)).astype(jnp.int32)
    start, end = bounds[:-1], bounds[1:]
    lo = start // blk
    hi = jnp.where(end > start, (end + blk - 1) // blk, lo)
    apply_norm = final_g is not None
    g = (final_g if apply_norm else jnp.ones((d,), F32)).reshape(1, d)
    return pl.pallas_call(
        functools.partial(_combine_kernel, tile=tile, blk=blk, apply_norm=apply_norm),
        grid=(n // tile,),
        in_specs=[pl.BlockSpec(memory_space=pltpu.SMEM), pl.BlockSpec(memory_space=pltpu.SMEM),
                  pl.BlockSpec((tile, d), lambda i: (i, 0)),
                  pl.BlockSpec((rows // blk, blk), lambda i: (0, 0)),
                  pl.BlockSpec((1, d), lambda i: (0, 0)),
                  pl.BlockSpec(memory_space=pl.ANY)],
        out_specs=pl.BlockSpec((tile, d), lambda i: (i, 0)),
        out_shape=jax.ShapeDtypeStruct((n, d), F32),
        scratch_shapes=[pltpu.VMEM((COMBINE_BUFFERS, blk, d), BF16),
                        pltpu.SemaphoreType.DMA((COMBINE_BUFFERS,))],
        compiler_params=_params("arbitrary"),
        name="combine",
    )(lo, hi, x2d, tok_sorted.reshape(rows // blk, blk), g, ye_sorted)


def _layer(x2d, bsz, seq, lw, moe_stacks, moe_cache, band_bias, cos, sin, final_g):
    n = bsz * seq
    layer = lw["layer"]
    if layer in moe_cache:
        proj, _ = norm_inproj(x2d, lw["norm_mix_g"], lw["w_in"])
    elif all(side_cast_fits(n, IN_WIDTH, st) for st in moe_stacks):
        proj, moe_cache[layer] = norm_inproj(x2d, lw["norm_mix_g"], lw["w_in"], moe_stacks, layer)
    else:
        proj, _ = norm_inproj(x2d, lw["norm_mix_g"], lw["w_in"])
        moe_cache[layer] = [layer_weight_bf16(st, layer) for st in moe_stacks]
    w_gate, w_up, w_down = moe_cache[layer]
    proj = proj.reshape(bsz, seq, IN_WIDTH)
    att = window_attention(proj, band_bias, lw["attn_sink"]).reshape(n, ATT_WIDTH)
    ret = retention(proj, lw["log_decay"], cos, sin).reshape(n, RET_WIDTH)
    x2d, hn, aff = outproj_norm_router(att, ret, x2d, lw["w_out"], lw["norm_ffn_g"], lw["w_router"])
    cap = CAPACITY_FACTOR * n // N_EXPERTS
    gates, idx = lax.top_k(aff, cap)
    ye = expert_ffn(hn[idx], gates[..., None], w_gate, w_up, w_down)
    slots = N_EXPERTS * cap
    tok_sorted, perm = lax.sort_key_val(idx.reshape(slots).astype(jnp.int32),
                                        jnp.arange(slots, dtype=jnp.int32))
    return combine(x2d, ye.reshape(slots, D_MODEL)[perm], tok_sorted, final_g)


def kernel(x_prompt, x_sample, norm_mix_g, w_in, attn_sink, rel_bias, ret_decay_fwd, ret_decay_bwd,
           w_out, norm_ffn_g, w_router, w_gate, w_up, w_down, norm_final_g):
    depth = w_in.shape[0]
    band_bias = _band_bias(rel_bias)
    layers = []
    for l in range(depth):
        layers.append({
            "layer": l,
            "norm_mix_g": norm_mix_g[l],
            "w_in": layer_weight_bf16(w_in, l),
            "attn_sink": attn_sink[l],
            "log_decay": jnp.stack([jax.nn.log_sigmoid(ret_decay_fwd[l].astype(F32)),
                                    jax.nn.log_sigmoid(ret_decay_bwd[l].astype(F32))]),
            "w_out": layer_weight_bf16(w_out, l),
            "norm_ffn_g": norm_ffn_g[l],
            "w_router": jnp.pad(w_router[l].astype(BF16), ((0, 0), (0, LANES - N_EXPERTS))),
        })
    moe_stacks = (w_gate, w_up, w_down)
    moe_cache = {}

    def trunk(x):
        bsz, seq, d = x.shape
        cos, sin = _rotary_tables(seq)
        x2d = x.reshape(bsz * seq, d)
        for l, lw in enumerate(layers):
            x2d = _layer(x2d, bsz, seq, lw, moe_stacks, moe_cache, band_bias, cos, sin,
                         norm_final_g if l == depth - 1 else None)
        return x2d.reshape(bsz, seq, d)

    return (trunk(x_prompt), trunk(x_sample))
```

```python
import functools
import math

import jax
import jax.numpy as jnp
import numpy as np
from jax import lax
from jax.experimental import pallas as pl
from jax.experimental.pallas import tpu as pltpu

D_MODEL = 2048
HEAD_DIM_ATT = 128
N_Q_HEADS = 8
N_KV_HEADS = 2
GQA = N_Q_HEADS // N_KV_HEADS
ATT_WIDTH = N_Q_HEADS * HEAD_DIM_ATT
KV_WIDTH = N_KV_HEADS * HEAD_DIM_ATT
WINDOW = 128
BLOCK = 128
N_BUCKETS = 32
MAX_DISTANCE = 128
N_RET_HEADS = 4
RET_HEAD_DIM = 256
RET_WIDTH = N_RET_HEADS * RET_HEAD_DIM
RET_CHUNK = 256
D_MIX = ATT_WIDTH + RET_WIDTH
IN_WIDTH = ATT_WIDTH + 2 * KV_WIDTH + 4 * RET_WIDTH
N_EXPERTS = 16
CAPACITY_FACTOR = 2
EXPERT_FF = D_MODEL // 2
EPS = 1e-6
NEG_INF = -1e30
LANES = 128
LOG2_E = math.log2(math.e)
SINK_ROWS = 16

VMEM_LIMIT_BYTES = 56 * 1024 * 1024

INPROJ_TILE = (512, 2816)
INPROJ_TILE_WITH_CAST = (1024, 1408)
ROW_TILE_OUTPROJ = 512
ATT_Q_TILE = 512
RET_SEG = 2048
MOE_SLOT_TILE = 512
GATE_ROWS = 8
COMBINE_TOKEN_TILE = 256
CAST_BLOCK_ELEMS = 1024 * 1024
SIDE_CAST_BLOCK_ELEMS = 512 * 1024
COMBINE_ROW_BLOCK = 256
COMBINE_BUFFERS = 4

BF16 = jnp.bfloat16
F32 = jnp.float32


def _params(*sem):
    return pltpu.CompilerParams(dimension_semantics=sem, vmem_limit_bytes=VMEM_LIMIT_BYTES)


def _cast_kernel(w_ref, o_ref):
    o_ref[...] = w_ref[0].astype(o_ref.dtype)


def layer_weight_bf16(w_stacked, layer):
    shape = w_stacked.shape[1:]
    cols = shape[-1]
    rows = math.prod(shape[:-1])
    tr = rows
    while tr * cols > CAST_BLOCK_ELEMS and tr % 32 == 0:
        tr //= 2
    out = pl.pallas_call(
        _cast_kernel,
        grid=(rows // tr,),
        in_specs=[pl.BlockSpec((1, tr, cols), lambda i: (layer, i, 0))],
        out_specs=pl.BlockSpec((tr, cols), lambda i: (i, 0)),
        out_shape=jax.ShapeDtypeStruct((rows, cols), BF16),
        compiler_params=_params("parallel"),
        name="weight_cast",
    )(w_stacked.reshape(w_stacked.shape[0], rows, cols))
    return out.reshape(shape)


def _norm_inproj_kernel(x_ref, g_ref, w_ref, *rest, n_cast):
    cast_in = rest[:n_cast]
    o_ref = rest[n_cast]
    cast_out = rest[n_cast + 1:2 * n_cast + 1]
    h_ref = rest[2 * n_cast + 1]

    @pl.when(pl.program_id(1) == 0)
    def _():
        x = x_ref[...]
        var = jnp.mean(x * x, axis=-1, keepdims=True)
        h_ref[...] = (x * lax.rsqrt(var + EPS) * g_ref[...]).astype(BF16)

    o_ref[...] = jnp.dot(h_ref[...], w_ref[...], preferred_element_type=F32).astype(o_ref.dtype)
    for src, dst in zip(cast_in, cast_out):
        dst[...] = src[0].astype(dst.dtype)


def _inproj_tiles(n, casting):
    tm, tn = INPROJ_TILE_WITH_CAST if casting else INPROJ_TILE
    return min(tm, n), tn


def side_cast_fits(n, width, stacked):
    tm, tn = _inproj_tiles(n, True)
    steps = (n // tm) * (width // tn)
    rows = math.prod(stacked.shape[1:-1])
    per_step = rows // steps
    return (rows % steps == 0 and per_step % 16 == 0
            and per_step * stacked.shape[-1] <= SIDE_CAST_BLOCK_ELEMS)


def norm_inproj(x2d, g, w_bf16, cast_stacks=(), cast_layer=0):
    n, d = x2d.shape
    width = w_bf16.shape[1]
    tm, tn = _inproj_tiles(n, bool(cast_stacks))
    ncol = width // tn
    steps = (n // tm) * ncol
    in_specs = [
        pl.BlockSpec((tm, d), lambda i, j: (i, 0)),
        pl.BlockSpec((1, d), lambda i, j: (0, 0)),
        pl.BlockSpec((d, tn), lambda i, j: (0, j)),
    ]
    out_specs = [pl.BlockSpec((tm, tn), lambda i, j: (i, j))]
    out_shape = [jax.ShapeDtypeStruct((n, width), BF16)]
    cast_args = []
    for st in cast_stacks:
        cols = st.shape[-1]
        rows = math.prod(st.shape[1:-1])
        per_step = rows // steps
        in_specs.append(pl.BlockSpec((1, per_step, cols), lambda i, j: (cast_layer, i * ncol + j, 0)))
        out_specs.append(pl.BlockSpec((per_step, cols), lambda i, j: (i * ncol + j, 0)))
        out_shape.append(jax.ShapeDtypeStruct((rows, cols), BF16))
        cast_args.append(st.reshape(st.shape[0], rows, cols))
    outs = pl.pallas_call(
        functools.partial(_norm_inproj_kernel, n_cast=len(cast_stacks)),
        grid=(n // tm, ncol),
        in_specs=in_specs,
        out_specs=out_specs,
        out_shape=out_shape,
        scratch_shapes=[pltpu.VMEM((tm, d), BF16)],
        compiler_params=_params("parallel", "arbitrary"),
        name="norm_inproj",
    )(x2d, g.reshape(1, d), w_bf16, *cast_args)
    return outs[0], [o.reshape(st.shape[1:]) for o, st in zip(outs[1:], cast_stacks)]


def _t5_bucket(rel):
    half = N_BUCKETS // 2
    max_exact = half // 2
    bucket = jnp.where(rel > 0, half, 0)
    n = jnp.abs(rel)
    nf = jnp.maximum(n, 1).astype(F32)
    large = max_exact + (jnp.log(nf / max_exact) / math.log(MAX_DISTANCE / max_exact)
                         * (half - max_exact)).astype(jnp.int32)
    large = jnp.minimum(large, half - 1)
    return bucket + jnp.where(n < max_exact, n, large)


def _band_bias(rel_bias):
    qi = jnp.arange(BLOCK)[:, None]
    kj = jnp.arange(3 * BLOCK)[None, :]
    rel = kj - BLOCK - qi
    onehot = jax.nn.one_hot(_t5_bucket(rel), N_BUCKETS, dtype=F32)
    bias = jnp.einsum('qkb,bh->qkh', onehot, rel_bias.astype(F32),
                      precision=lax.Precision.HIGHEST)
    bias = jnp.where((jnp.abs(rel) <= WINDOW)[:, :, None], bias, NEG_INF)
    bias = bias.reshape(BLOCK, 3, BLOCK, N_KV_HEADS, GQA).transpose(3, 1, 2, 4, 0)
    bias = bias.reshape(N_KV_HEADS, 3, BLOCK, GQA * BLOCK)
    return jnp.concatenate([bias, jnp.full_like(bias[:, :1], NEG_INF)], axis=1) * LOG2_E


def _attn_kernel(q_ref, kp_ref, km_ref, kn_ref, vp_ref, vm_ref, vn_ref, bias_ref, sink_ref,
                 o_ref, k_scr, v_scr, *, tq):
    i = pl.program_id(1)
    last = pl.num_programs(1) - 1
    k_scr[0:BLOCK] = kp_ref[0]
    k_scr[BLOCK:BLOCK + tq] = km_ref[0]
    k_scr[BLOCK + tq:] = kn_ref[0]
    v_scr[0:BLOCK] = vp_ref[0]
    v_scr[BLOCK:BLOCK + tq] = vm_ref[0]
    v_scr[BLOCK + tq:] = vn_ref[0]
    scale = HEAD_DIM_ATT ** -0.5 * LOG2_E
    nsb = tq // BLOCK
    nt = (((1,), (1,)), ((), ()))
    tn = (((0,), (0,)), ((), ()))
    ones = jnp.ones((3 * BLOCK, HEAD_DIM_ATT), BF16)
    r_id = lax.broadcasted_iota(jnp.int32, (SINK_ROWS, 2 * HEAD_DIM_ATT), 0)
    c_id = lax.broadcasted_iota(jnp.int32, (SINK_ROWS, 2 * HEAD_DIM_ATT), 1)
    sink_rows = jnp.where((r_id == 0) & (c_id >= HEAD_DIM_ATT), 1.0, 0.0).astype(BF16)
    for sb in range(nsb):
        for h in range(N_KV_HEADS):
            c0 = h * HEAD_DIM_ATT
            qs = jnp.concatenate(
                [q_ref[0, sb * BLOCK:(sb + 1) * BLOCK,
                       (h * GQA + g) * HEAD_DIM_ATT:(h * GQA + g + 1) * HEAD_DIM_ATT]
                 for g in range(GQA)], axis=0)
            parts = []
            for c in range(3):
                kc = k_scr[(sb + c) * BLOCK:(sb + c + 1) * BLOCK, c0:c0 + HEAD_DIM_ATT]
                if c == 0 and sb == 0:
                    slot = jnp.where(i == 0, 3, 0)
                elif c == 2 and sb == nsb - 1:
                    slot = jnp.where(i == last, 3, 2)
                else:
                    slot = c
                s_c = lax.dot_general(kc, qs, nt, preferred_element_type=F32)
                parts.append(s_c * scale + bias_ref[h, slot])
            s = jnp.concatenate(parts, axis=0)
            sink = sink_ref[h:h + 1, :]
            m = jnp.maximum(jnp.max(s, axis=0, keepdims=True), sink)
            p = jnp.exp2(s - m).astype(BF16)
            p_sink = jnp.broadcast_to(jnp.exp2(sink - m).astype(BF16), (SINK_ROWS, GQA * BLOCK))
            p_ext = jnp.concatenate([p, p_sink], axis=0)
            vb = v_scr[sb * BLOCK:(sb + 3) * BLOCK, c0:c0 + HEAD_DIM_ATT]
            v_ext = jnp.concatenate([jnp.concatenate([vb, ones], axis=1), sink_rows], axis=0)
            o_ext = lax.dot_general(p_ext, v_ext, tn, preferred_element_type=F32)
            o = o_ext[:, :HEAD_DIM_ATT] / o_ext[:, HEAD_DIM_ATT:]
            for g in range(GQA):
                o_ref[0, sb * BLOCK:(sb + 1) * BLOCK,
                      (h * GQA + g) * HEAD_DIM_ATT:(h * GQA + g + 1) * HEAD_DIM_ATT] = (
                    o[g * BLOCK:(g + 1) * BLOCK].astype(o_ref.dtype))


def window_attention(proj, band_bias, sink):
    b, s, _ = proj.shape
    tq = min(ATT_Q_TILE, s)
    nblk = s // BLOCK
    r = tq // BLOCK
    assert r >= 2 and s % tq == 0
    k_col = ATT_WIDTH // KV_WIDTH
    v_col = k_col + 1

    def main(col):
        return pl.BlockSpec((1, tq, KV_WIDTH), lambda bi, i: (bi, i, col))

    def prev(col):
        return pl.BlockSpec((1, BLOCK, KV_WIDTH), lambda bi, i: (bi, jnp.maximum(i * r - 1, 0), col))

    def nxt(col):
        return pl.BlockSpec((1, BLOCK, KV_WIDTH),
                            lambda bi, i: (bi, jnp.minimum((i + 1) * r, nblk - 1), col))

    sink_rows = jnp.repeat(sink.astype(F32) * LOG2_E, BLOCK).reshape(N_KV_HEADS, GQA * BLOCK)
    return pl.pallas_call(
        functools.partial(_attn_kernel, tq=tq),
        grid=(b, s // tq),
        in_specs=[
            pl.BlockSpec((1, tq, ATT_WIDTH), lambda bi, i: (bi, i, 0)),
            prev(k_col), main(k_col), nxt(k_col),
            prev(v_col), main(v_col), nxt(v_col),
            pl.BlockSpec((N_KV_HEADS, 4, BLOCK, GQA * BLOCK), lambda bi, i: (0, 0, 0, 0)),
            pl.BlockSpec((N_KV_HEADS, GQA * BLOCK), lambda bi, i: (0, 0)),
        ],
        out_specs=pl.BlockSpec((1, tq, ATT_WIDTH), lambda bi, i: (bi, i, 0)),
        out_shape=jax.ShapeDtypeStruct((b, s, ATT_WIDTH), BF16),
        scratch_shapes=[pltpu.VMEM((tq + 2 * BLOCK, KV_WIDTH), BF16),
                        pltpu.VMEM((tq + 2 * BLOCK, KV_WIDTH), BF16)],
        compiler_params=_params("parallel", "arbitrary"),
        name="window_attention",
    )(proj, proj, proj, proj, proj, proj, proj, band_bias, sink_rows)


def _rotate(x, cos, sin):
    half = RET_HEAD_DIM // 2
    x1 = x[:, :half]
    x2 = x[:, half:]
    return jnp.concatenate([x1 * cos - x2 * sin, x1 * sin + x2 * cos], axis=-1)


def _ret_kernel(ld_ref, q_ref, k_ref, v_ref, g_ref, cos_ref, sin_ref, o_ref,
                oacc, qrot, kbwd, kv_scr, rprev, rstate, *, nseg, seg_len):
    h = pl.program_id(1)
    s = pl.program_id(2)
    cs = RET_CHUNK
    nchunk = seg_len // cs
    ld_f = ld_ref[0, h]
    ld_b = ld_ref[1, h]
    ii = lax.broadcasted_iota(jnp.int32, (cs, 1), 0).astype(F32)
    jj = lax.broadcasted_iota(jnp.int32, (1, cs), 1).astype(F32)
    k_scale = RET_HEAD_DIM ** -0.5
    nt = (((1,), (1,)), ((), ()))
    tn = (((0,), (0,)), ((), ()))
    chunk_len = jnp.full((1, 1), cs, F32)

    @pl.when(s == 0)
    def _():
        rstate[...] = jnp.zeros_like(rstate)

    @pl.when(s < nseg)
    def _():
        base = s * seg_len
        diff = ii - jj
        dmat = k_scale * jnp.where(diff >= 0, jnp.exp(ld_f * jnp.maximum(diff, 0.0)),
                                   jnp.exp(ld_b * jnp.maximum(-diff, 0.0)))
        q_dec = jnp.exp(ld_f * (ii + 1.0))
        k_dec_f = k_scale * jnp.exp(ld_f * (cs - 1.0 - ii))
        k_dec_b = k_scale * jnp.exp(ld_b * ii)
        g_chunk = jnp.exp(chunk_len * ld_f)

        def products(c, carry):
            r0 = pl.multiple_of(c * cs, cs)
            rows = pl.ds(pl.multiple_of(base + r0, cs), cs)
            cos = cos_ref[pl.ds(r0, cs), :]
            sin = sin_ref[pl.ds(r0, cs), :]
            q = _rotate(q_ref[0, pl.ds(r0, cs), :].astype(F32), cos, sin).astype(BF16)
            k = _rotate(k_ref[0, pl.ds(r0, cs), :].astype(F32), cos, sin)
            v = v_ref[0, pl.ds(r0, cs), :]
            qrot[rows, :] = q
            kbwd[rows, :] = (k * k_dec_b).astype(BF16)
            a = lax.dot_general(q, k.astype(BF16), nt, preferred_element_type=F32) * dmat
            oacc[rows, :] = jnp.dot(a.astype(BF16), v, preferred_element_type=F32)
            kv_scr[c] = lax.dot_general((k * k_dec_f).astype(BF16), v, tn, preferred_element_type=F32)
            return carry

        for c_static in range(nchunk):
            products(c_static, 0)

        def scan(c, carry):
            r = rstate[0]
            rprev[c] = r.astype(BF16)
            rstate[0] = g_chunk * r + kv_scr[c]
            return carry

        for c_static in range(nchunk):
            scan(c_static, 0)

        def cross(c, carry):
            rows = pl.ds(pl.multiple_of(base + c * cs, cs), cs)
            oacc[rows, :] += q_dec * jnp.dot(qrot[rows, :], rprev[c], preferred_element_type=F32)
            return carry

        for c_static in range(nchunk):
            cross(c_static, 0)

    @pl.when(s >= nseg)
    def _():
        base = (2 * nseg - 1 - s) * seg_len
        q_dec = jnp.exp(ld_b * (cs - ii))
        g_chunk = jnp.exp(chunk_len * ld_b)

        def products(c, carry):
            r0 = pl.multiple_of(c * cs, cs)
            rows = pl.ds(pl.multiple_of(base + r0, cs), cs)
            kv_scr[c] = lax.dot_general(kbwd[rows, :], v_ref[0, pl.ds(r0, cs), :], tn,
                                        preferred_element_type=F32)
            return carry

        for c_static in range(nchunk):
            products(c_static, 0)

        def scan(t, carry):
            c = nchunk - 1 - t
            r = rstate[1]
            rprev[c] = r.astype(BF16)
            rstate[1] = g_chunk * r + kv_scr[c]
            return carry

        for c_static in range(nchunk):
            scan(c_static, 0)

        def finish(c, carry):
            r0 = pl.multiple_of(c * cs, cs)
            rows = pl.ds(pl.multiple_of(base + r0, cs), cs)
            o = oacc[rows, :] + q_dec * jnp.dot(qrot[rows, :], rprev[c], preferred_element_type=F32)
            mu = jnp.mean(o, axis=-1, keepdims=True)
            oc = o - mu
            var = jnp.mean(oc * oc, axis=-1, keepdims=True)
            gate = g_ref[0, pl.ds(r0, cs), :].astype(F32)
            silu = gate * (1.0 / (1.0 + jnp.exp(-gate)))
            o_ref[0, pl.ds(r0, cs), :] = (oc * lax.rsqrt(var + EPS) * silu).astype(o_ref.dtype)
            return carry

        for c_static in range(nchunk):
            finish(c_static, 0)


def _rotary_tables(seq_len):
    half = RET_HEAD_DIM // 2
    inv = 1.0 / (10000.0 ** jnp.linspace(0.0, 1.0, half, dtype=F32))
    ang = jnp.arange(seq_len, dtype=F32)[:, None] * inv[None, :]
    return jnp.cos(ang), jnp.sin(ang)


def retention(proj, log_decay, cos, sin):
    b, s, _ = proj.shape
    seg_len = min(RET_SEG, s)
    nseg = s // seg_len
    nchunk = seg_len // RET_CHUNK
    base = (ATT_WIDTH + 2 * KV_WIDTH) // RET_HEAD_DIM

    def seg_both(t):
        return jnp.where(t < nseg, t, 2 * nseg - 1 - t)

    def seg_fwd_only(t):
        return jnp.minimum(t, nseg - 1)

    def seg_bwd_only(t):
        return jnp.where(t < nseg, nseg - 1, 2 * nseg - 1 - t)

    def head_block(which, seg):
        off = base + which * N_RET_HEADS
        return pl.BlockSpec((1, seg_len, RET_HEAD_DIM), lambda bi, h, t: (bi, seg(t), off + h))

    table = pl.BlockSpec((seg_len, RET_HEAD_DIM // 2), lambda bi, h, t: (seg_fwd_only(t), 0))
    return pl.pallas_call(
        functools.partial(_ret_kernel, nseg=nseg, seg_len=seg_len),
        grid=(b, N_RET_HEADS, 2 * nseg),
        in_specs=[pl.BlockSpec(memory_space=pltpu.SMEM),
                  head_block(0, seg_fwd_only), head_block(1, seg_fwd_only),
                  head_block(2, seg_both), head_block(3, seg_bwd_only), table, table],
        out_specs=pl.BlockSpec((1, seg_len, RET_HEAD_DIM),
                               lambda bi, h, t: (bi, seg_bwd_only(t), h)),
        out_shape=jax.ShapeDtypeStruct((b, s, RET_WIDTH), BF16),
        scratch_shapes=[pltpu.VMEM((s, RET_HEAD_DIM), F32),
                        pltpu.VMEM((s, RET_HEAD_DIM), BF16),
                        pltpu.VMEM((s, RET_HEAD_DIM), BF16),
                        pltpu.VMEM((nchunk, RET_HEAD_DIM, RET_HEAD_DIM), F32),
                        pltpu.VMEM((nchunk, RET_HEAD_DIM, RET_HEAD_DIM), BF16),
                        pltpu.VMEM((2, RET_HEAD_DIM, RET_HEAD_DIM), F32)],
        compiler_params=_params("parallel", "arbitrary", "arbitrary"),
        name="retention",
    )(log_decay, proj, proj, proj, proj, cos, sin)


def _outproj_kernel(att_ref, ret_ref, x_ref, wa_ref, wr_ref, g_ref, wrt_ref, xo_ref, hn_ref, aff_ref, ybuf):
    i = pl.program_id(0)
    cur = lax.rem(i, 2)

    @pl.when(i == 0)
    def _():
        ybuf[1] = jnp.zeros(ybuf.shape[1:], F32)

    y_prev = ybuf[1 - cur]
    var = jnp.mean(y_prev * y_prev, axis=-1, keepdims=True)
    hn = (y_prev * lax.rsqrt(var + EPS) * g_ref[...]).astype(BF16)
    hn_ref[...] = hn
    logits = jnp.dot(hn, wrt_ref[...], preferred_element_type=F32)
    lane = lax.broadcasted_iota(jnp.int32, (1, LANES), 1)
    logits = jnp.where(lane < N_EXPERTS, logits, NEG_INF)
    e = jnp.exp(logits - jnp.max(logits, axis=-1, keepdims=True))
    aff = e / jnp.sum(e, axis=-1, keepdims=True)
    aff_ref[...] = jnp.transpose(aff)[:N_EXPERTS]

    y = x_ref[...] + jnp.dot(att_ref[...], wa_ref[...], preferred_element_type=F32)
    y = y + jnp.dot(ret_ref[...], wr_ref[...], preferred_element_type=F32)
    xo_ref[...] = y
    ybuf[cur] = y


def outproj_norm_router(att, ret, x2d, w_out_bf16, g, w_router_pad):
    n, d = x2d.shape
    tm = min(ROW_TILE_OUTPROJ, n)
    nt = n // tm
    row = lambda i: (jnp.minimum(i, nt - 1), 0)
    prev_row = lambda i: (jnp.maximum(i - 1, 0), 0)
    return pl.pallas_call(
        _outproj_kernel,
        grid=(nt + 1,),
        in_specs=[
            pl.BlockSpec((tm, ATT_WIDTH), row),
            pl.BlockSpec((tm, RET_WIDTH), row),
            pl.BlockSpec((tm, d), row),
            pl.BlockSpec((ATT_WIDTH, d), lambda i: (0, 0)),
            pl.BlockSpec((RET_WIDTH, d), lambda i: (1, 0)),
            pl.BlockSpec((1, d), lambda i: (0, 0)),
            pl.BlockSpec((d, LANES), lambda i: (0, 0)),
        ],
        out_specs=[pl.BlockSpec((tm, d), row), pl.BlockSpec((tm, d), prev_row),
                   pl.BlockSpec((N_EXPERTS, tm), lambda i: (0, jnp.maximum(i - 1, 0)))],
        out_shape=[jax.ShapeDtypeStruct((n, d), F32), jax.ShapeDtypeStruct((n, d), BF16),
                   jax.ShapeDtypeStruct((N_EXPERTS, n), F32)],
        scratch_shapes=[pltpu.VMEM((2, tm, d), F32)],
        compiler_params=_params("arbitrary"),
        name="outproj_norm_router",
    )(att, ret, x2d, w_out_bf16, w_out_bf16, g.reshape(1, d), w_router_pad)


def _moe_kernel(xe_ref, wg_ref, wu_ref, wd_ref, gate_ref, o_ref):
    x = xe_ref[0]
    hg = jnp.dot(x, wg_ref[0], preferred_element_type=F32)
    hu = jnp.dot(x, wu_ref[0], preferred_element_type=F32)
    h = hg * (1.0 / (1.0 + jnp.exp(-hg))) * hu
    y = jnp.dot(h.astype(BF16), wd_ref[0], preferred_element_type=F32)
    gate = jnp.transpose(gate_ref[0, 0])[:, 0:1]
    o_ref[0] = (y * gate).astype(o_ref.dtype)


def expert_ffn(xe, gates, wg, wu, wd):
    e, cap, d = xe.shape
    ff = wg.shape[-1]
    tc = min(MOE_SLOT_TILE, cap)
    return pl.pallas_call(
        _moe_kernel,
        grid=(e, cap // tc),
        in_specs=[
            pl.BlockSpec((1, tc, d), lambda ei, t: (ei, t, 0)),
            pl.BlockSpec((1, d, ff), lambda ei, t: (ei, 0, 0)),
            pl.BlockSpec((1, d, ff), lambda ei, t: (ei, 0, 0)),
            pl.BlockSpec((1, ff, d), lambda ei, t: (ei, 0, 0)),
            pl.BlockSpec((1, 1, GATE_ROWS, tc), lambda ei, t: (ei, t, 0, 0)),
        ],
        out_specs=pl.BlockSpec((1, tc, d), lambda ei, t: (ei, t, 0)),
        out_shape=jax.ShapeDtypeStruct((e, cap, d), BF16),
        compiler_params=_params("parallel", "arbitrary"),
        name="expert_ffn",
    )(xe, wg, wu, wd, jnp.broadcast_to(gates.reshape(e, cap // tc, 1, tc), (e, cap // tc, GATE_ROWS, tc)))


def _combine_kernel(lo_ref, hi_ref, x_ref, tok_ref, g_ref, ye_hbm, o_ref, buf, sem, *, tile, blk, apply_norm):
    i = pl.program_id(0)
    lo = lo_ref[i]
    hi = hi_ref[i]
    t0 = i * tile
    ahead = COMBINE_BUFFERS - 1

    def fetch(c):
        slot = lax.rem(c, COMBINE_BUFFERS)
        return pltpu.make_async_copy(ye_hbm.at[pl.ds(pl.multiple_of(c * blk, blk), blk)],
                                     buf.at[slot], sem.at[slot])

    def start_first_blocks(first, end):
        for k in range(ahead):
            @pl.when(first + k < end)
            def _():
                fetch(first + k).start()

    @pl.when(i == 0)
    def _():
        start_first_blocks(lo, hi)

    o_ref[...] = x_ref[...]
    row = lax.broadcasted_iota(jnp.int32, (tile, blk), 0)

    def body(c, carry):
        fetch(c).wait()

        @pl.when(c + ahead < hi)
        def _():
            fetch(c + ahead).start()

        rel = tok_ref[pl.ds(c, 1), :] - t0
        sel = jnp.where(row == rel, 1.0, 0.0).astype(BF16)
        o_ref[...] += jnp.dot(sel, buf[lax.rem(c, COMBINE_BUFFERS)], preferred_element_type=F32)
        return carry

    lax.fori_loop(lo, hi, body, 0)

    @pl.when(i + 1 < pl.num_programs(0))
    def _():
        start_first_blocks(lo_ref[i + 1], hi_ref[i + 1])

    if apply_norm:
        y = o_ref[...]
        var = jnp.mean(y * y, axis=-1, keepdims=True)
        o_ref[...] = y * lax.rsqrt(var + EPS) * g_ref[...]


def combine(x2d, ye_sorted, tok_sorted, final_g):
    n, d = x2d.shape
    rows = ye_sorted.shape[0]
    tile = min(COMBINE_TOKEN_TILE, n)
    blk = min(COMBINE_ROW_BLOCK, rows)
    bounds = jnp.searchsorted(tok_sorted, jnp.arange(0, n + 1, tile, dtype=jnp.int32),
                              method="compare_all").astype(jnp.int32)
    start, end = bounds[:-1], bounds[1:]
    lo = start // blk
    hi = jnp.where(end > start, (end + blk - 1) // blk, lo)
    apply_norm = final_g is not None
    g = (final_g if apply_norm else jnp.ones((d,), F32)).reshape(1, d)
    return pl.pallas_call(
        functools.partial(_combine_kernel, tile=tile, blk=blk, apply_norm=apply_norm),
        grid=(n // tile,),
        in_specs=[pl.BlockSpec(memory_space=pltpu.SMEM), pl.BlockSpec(memory_space=pltpu.SMEM),
                  pl.BlockSpec((tile, d), lambda i: (i, 0)),
                  pl.BlockSpec((rows // blk, blk), lambda i: (0, 0)),
                  pl.BlockSpec((1, d), lambda i: (0, 0)),
                  pl.BlockSpec(memory_space=pl.ANY)],
        out_specs=pl.BlockSpec((tile, d), lambda i: (i, 0)),
        out_shape=jax.ShapeDtypeStruct((n, d), F32),
        scratch_shapes=[pltpu.VMEM((COMBINE_BUFFERS, blk, d), BF16),
                        pltpu.SemaphoreType.DMA((COMBINE_BUFFERS,))],
        compiler_params=_params("arbitrary"),
        name="combine",
    )(lo, hi, x2d, tok_sorted.reshape(rows // blk, blk), g, ye_sorted)


def _layer(x2d, bsz, seq, lw, moe_stacks, moe_cache, band_bias, cos, sin, final_g):
    n = bsz * seq
    layer = lw["layer"]
    if layer in moe_cache:
        proj, _ = norm_inproj(x2d, lw["norm_mix_g"], lw["w_in"])
    elif all(side_cast_fits(n, IN_WIDTH, st) for st in moe_stacks):
        proj, moe_cache[layer] = norm_inproj(x2d, lw["norm_mix_g"], lw["w_in"], moe_stacks, layer)
    else:
        proj, _ = norm_inproj(x2d, lw["norm_mix_g"], lw["w_in"])
        moe_cache[layer] = [layer_weight_bf16(st, layer) for st in moe_stacks]
    w_gate, w_up, w_down = moe_cache[layer]
    proj = proj.reshape(bsz, seq, IN_WIDTH)
    att = window_attention(proj, band_bias, lw["attn_sink"]).reshape(n, ATT_WIDTH)
    ret = retention(proj, lw["log_decay"], cos, sin).reshape(n, RET_WIDTH)
    x2d, hn, aff = outproj_norm_router(att, ret, x2d, lw["w_out"], lw["norm_ffn_g"], lw["w_router"])
    cap = CAPACITY_FACTOR * n // N_EXPERTS
    gates, idx = lax.top_k(aff, cap)
    ye = expert_ffn(hn[idx], gates, w_gate, w_up, w_down)
    slots = N_EXPERTS * cap
    tok_sorted, perm = lax.sort_key_val(idx.reshape(slots).astype(jnp.int32),
                                        jnp.arange(slots, dtype=jnp.int32))
    return combine(x2d, ye.reshape(slots, D_MODEL)[perm], tok_sorted, final_g)


def kernel(x_prompt, x_sample, norm_mix_g, w_in, attn_sink, rel_bias, ret_decay_fwd, ret_decay_bwd,
           w_out, norm_ffn_g, w_router, w_gate, w_up, w_down, norm_final_g):
    depth = w_in.shape[0]
    band_bias = _band_bias(rel_bias)
    layers = []
    for l in range(depth):
        layers.append({
            "layer": l,
            "norm_mix_g": norm_mix_g[l],
            "w_in": layer_weight_bf16(w_in, l),
            "attn_sink": attn_sink[l],
            "log_decay": jnp.stack([jax.nn.log_sigmoid(ret_decay_fwd[l].astype(F32)),
                                    jax.nn.log_sigmoid(ret_decay_bwd[l].astype(F32))]),
            "w_out": layer_weight_bf16(w_out, l),
            "norm_ffn_g": norm_ffn_g[l],
            "w_router": jnp.pad(w_router[l].astype(BF16), ((0, 0), (0, LANES - N_EXPERTS))),
        })
    moe_stacks = (w_gate, w_up, w_down)
    moe_cache = {}

    def trunk(x):
        bsz, seq, d = x.shape
        cos, sin = _rotary_tables(seq)
        x2d = x.reshape(bsz * seq, d)
        for l, lw in enumerate(layers):
            x2d = _layer(x2d, bsz, seq, lw, moe_stacks, moe_cache, band_bias, cos, sin,
                         norm_final_g if l == depth - 1 else None)
        return x2d.reshape(bsz, seq, d)

    return (trunk(x_prompt), trunk(x_sample))
```

```python
import functools
import math

import jax
import jax.numpy as jnp
import numpy as np
from jax import lax
from jax.experimental import pallas as pl
from jax.experimental.pallas import tpu as pltpu

D_MODEL = 2048
HEAD_DIM_ATT = 128
N_Q_HEADS = 8
N_KV_HEADS = 2
GQA = N_Q_HEADS // N_KV_HEADS
ATT_WIDTH = N_Q_HEADS * HEAD_DIM_ATT
KV_WIDTH = N_KV_HEADS * HEAD_DIM_ATT
WINDOW = 128
BLOCK = 128
N_BUCKETS = 32
MAX_DISTANCE = 128
N_RET_HEADS = 4
RET_HEAD_DIM = 256
RET_WIDTH = N_RET_HEADS * RET_HEAD_DIM
RET_CHUNK = 256
D_MIX = ATT_WIDTH + RET_WIDTH
IN_WIDTH = ATT_WIDTH + 2 * KV_WIDTH + 4 * RET_WIDTH
N_EXPERTS = 16
CAPACITY_FACTOR = 2
EXPERT_FF = D_MODEL // 2
EPS = 1e-6
NEG_INF = -1e30
LANES = 128
LOG2_E = math.log2(math.e)
SINK_ROWS = 16

VMEM_LIMIT_BYTES = 56 * 1024 * 1024

INPROJ_ROW_TILE = 512
INPROJ_ROW_TILE_WITH_CAST = 256
ROW_TILE_OUTPROJ = 512
ATT_Q_TILE = 512
RET_SEG = 2048
MOE_SLOT_TILE = 512
GATE_ROWS = 8
COMBINE_TOKEN_TILE = 256
CAST_BLOCK_ELEMS = 1024 * 1024
SIDE_CAST_BLOCK_ELEMS = 512 * 1024
COMBINE_ROW_BLOCK = 256
COMBINE_BUFFERS = 4

BF16 = jnp.bfloat16
F32 = jnp.float32


def _params(*sem):
    return pltpu.CompilerParams(dimension_semantics=sem, vmem_limit_bytes=VMEM_LIMIT_BYTES)


def _cast_kernel(w_ref, o_ref):
    o_ref[...] = w_ref[0].astype(o_ref.dtype)


def layer_weight_bf16(w_stacked, layer):
    shape = w_stacked.shape[1:]
    cols = shape[-1]
    rows = math.prod(shape[:-1])
    tr = rows
    while tr * cols > CAST_BLOCK_ELEMS and tr % 32 == 0:
        tr //= 2
    out = pl.pallas_call(
        _cast_kernel,
        grid=(rows // tr,),
        in_specs=[pl.BlockSpec((1, tr, cols), lambda i: (layer, i, 0))],
        out_specs=pl.BlockSpec((tr, cols), lambda i: (i, 0)),
        out_shape=jax.ShapeDtypeStruct((rows, cols), BF16),
        compiler_params=_params("parallel"),
        name="weight_cast",
    )(w_stacked.reshape(w_stacked.shape[0], rows, cols))
    return out.reshape(shape)


def _norm_inproj_kernel(x_ref, g_ref, w_ref, *rest, n_cast):
    cast_in = rest[:n_cast]
    o_ref = rest[n_cast]
    cast_out = rest[n_cast + 1:2 * n_cast + 1]
    hbuf = rest[2 * n_cast + 1]
    i = pl.program_id(0)

    @pl.when(i == 0)
    def _():
        hbuf[1] = jnp.zeros(hbuf.shape[1:], BF16)

    def step(cur):
        o_ref[...] = jnp.dot(hbuf[1 - cur], w_ref[...], preferred_element_type=F32).astype(o_ref.dtype)
        x = x_ref[...]
        var = jnp.mean(x * x, axis=-1, keepdims=True)
        hbuf[cur] = (x * lax.rsqrt(var + EPS) * g_ref[...]).astype(BF16)
        for src, dst in zip(cast_in, cast_out):
            dst[...] = src[0].astype(dst.dtype)

    @pl.when(lax.rem(i, 2) == 0)
    def _():
        step(0)

    @pl.when(lax.rem(i, 2) == 1)
    def _():
        step(1)


def _inproj_row_tile(n, casting):
    return min(INPROJ_ROW_TILE_WITH_CAST if casting else INPROJ_ROW_TILE, n)


def side_cast_fits(n, stacked):
    steps = n // _inproj_row_tile(n, True)
    rows = math.prod(stacked.shape[1:-1])
    per_step = rows // steps
    return (rows % steps == 0 and per_step % 16 == 0
            and per_step * stacked.shape[-1] <= SIDE_CAST_BLOCK_ELEMS)


def norm_inproj(x2d, g, w_bf16, cast_stacks=(), cast_layer=0):
    n, d = x2d.shape
    width = w_bf16.shape[1]
    tm = _inproj_row_tile(n, bool(cast_stacks))
    nt = n // tm
    cur_row = lambda i: (jnp.minimum(i, nt - 1), 0)
    prev_row = lambda i: (jnp.maximum(i - 1, 0), 0)
    in_specs = [
        pl.BlockSpec((tm, d), cur_row),
        pl.BlockSpec((1, d), lambda i: (0, 0)),
        pl.BlockSpec((d, width), lambda i: (0, 0), pipeline_mode=pl.Buffered(1)),
    ]
    out_specs = [pl.BlockSpec((tm, width), prev_row)]
    out_shape = [jax.ShapeDtypeStruct((n, width), BF16)]
    cast_args = []
    for st in cast_stacks:
        cols = st.shape[-1]
        rows = math.prod(st.shape[1:-1])
        per_step = rows // nt
        in_specs.append(pl.BlockSpec((1, per_step, cols), lambda i: (cast_layer, jnp.minimum(i, nt - 1), 0)))
        out_specs.append(pl.BlockSpec((per_step, cols), lambda i: (jnp.minimum(i, nt - 1), 0)))
        out_shape.append(jax.ShapeDtypeStruct((rows, cols), BF16))
        cast_args.append(st.reshape(st.shape[0], rows, cols))
    outs = pl.pallas_call(
        functools.partial(_norm_inproj_kernel, n_cast=len(cast_stacks)),
        grid=(nt + 1,),
        in_specs=in_specs,
        out_specs=out_specs,
        out_shape=out_shape,
        scratch_shapes=[pltpu.VMEM((2, tm, d), BF16)],
        compiler_params=_params("arbitrary"),
        name="norm_inproj",
    )(x2d, g.reshape(1, d), w_bf16, *cast_args)
    return outs[0], [o.reshape(st.shape[1:]) for o, st in zip(outs[1:], cast_stacks)]


def _t5_bucket(rel):
    half = N_BUCKETS // 2
    max_exact = half // 2
    bucket = jnp.where(rel > 0, half, 0)
    n = jnp.abs(rel)
    nf = jnp.maximum(n, 1).astype(F32)
    large = max_exact + (jnp.log(nf / max_exact) / math.log(MAX_DISTANCE / max_exact)
                         * (half - max_exact)).astype(jnp.int32)
    large = jnp.minimum(large, half - 1)
    return bucket + jnp.where(n < max_exact, n, large)


def _band_bias(rel_bias):
    qi = jnp.arange(BLOCK)[:, None]
    kj = jnp.arange(3 * BLOCK)[None, :]
    rel = kj - BLOCK - qi
    onehot = jax.nn.one_hot(_t5_bucket(rel), N_BUCKETS, dtype=F32)
    bias = jnp.einsum('qkb,bh->qkh', onehot, rel_bias.astype(F32),
                      precision=lax.Precision.HIGHEST)
    bias = jnp.where((jnp.abs(rel) <= WINDOW)[:, :, None], bias, NEG_INF)
    bias = bias.reshape(BLOCK, 3, BLOCK, N_KV_HEADS, GQA).transpose(3, 1, 2, 4, 0)
    bias = bias.reshape(N_KV_HEADS, 3, BLOCK, GQA * BLOCK)
    return jnp.concatenate([bias, jnp.full_like(bias[:, :1], NEG_INF)], axis=1) * LOG2_E


def _attn_kernel(q_ref, kp_ref, km_ref, kn_ref, vp_ref, vm_ref, vn_ref, bias_ref, sink_ref,
                 o_ref, k_scr, v_scr, *, tq):
    i = pl.program_id(1)
    last = pl.num_programs(1) - 1
    k_scr[0:BLOCK] = kp_ref[0]
    k_scr[BLOCK:BLOCK + tq] = km_ref[0]
    k_scr[BLOCK + tq:] = kn_ref[0]
    v_scr[0:BLOCK] = vp_ref[0]
    v_scr[BLOCK:BLOCK + tq] = vm_ref[0]
    v_scr[BLOCK + tq:] = vn_ref[0]
    scale = HEAD_DIM_ATT ** -0.5 * LOG2_E
    nsb = tq // BLOCK
    nt = (((1,), (1,)), ((), ()))
    tn = (((0,), (0,)), ((), ()))
    ones = jnp.ones((3 * BLOCK, HEAD_DIM_ATT), BF16)
    r_id = lax.broadcasted_iota(jnp.int32, (SINK_ROWS, 2 * HEAD_DIM_ATT), 0)
    c_id = lax.broadcasted_iota(jnp.int32, (SINK_ROWS, 2 * HEAD_DIM_ATT), 1)
    sink_rows = jnp.where((r_id == 0) & (c_id >= HEAD_DIM_ATT), 1.0, 0.0).astype(BF16)
    for sb in range(nsb):
        for h in range(N_KV_HEADS):
            c0 = h * HEAD_DIM_ATT
            qs = jnp.concatenate(
                [q_ref[0, sb * BLOCK:(sb + 1) * BLOCK,
                       (h * GQA + g) * HEAD_DIM_ATT:(h * GQA + g + 1) * HEAD_DIM_ATT]
                 for g in range(GQA)], axis=0)
            parts = []
            for c in range(3):
                kc = k_scr[(sb + c) * BLOCK:(sb + c + 1) * BLOCK, c0:c0 + HEAD_DIM_ATT]
                if c == 0 and sb == 0:
                    slot = jnp.where(i == 0, 3, 0)
                elif c == 2 and sb == nsb - 1:
                    slot = jnp.where(i == last, 3, 2)
                else:
                    slot = c
                s_c = lax.dot_general(kc, qs, nt, preferred_element_type=F32)
                parts.append(s_c * scale + bias_ref[h, slot])
            s = jnp.concatenate(parts, axis=0)
            sink = sink_ref[h:h + 1, :]
            m = jnp.maximum(jnp.max(s, axis=0, keepdims=True), sink)
            p = jnp.exp2(s - m).astype(BF16)
            p_sink = jnp.broadcast_to(jnp.exp2(sink - m).astype(BF16), (SINK_ROWS, GQA * BLOCK))
            p_ext = jnp.concatenate([p, p_sink], axis=0)
            vb = v_scr[sb * BLOCK:(sb + 3) * BLOCK, c0:c0 + HEAD_DIM_ATT]
            v_ext = jnp.concatenate([jnp.concatenate([vb, ones], axis=1), sink_rows], axis=0)
            o_ext = lax.dot_general(p_ext, v_ext, tn, preferred_element_type=F32)
            o = o_ext[:, :HEAD_DIM_ATT] / o_ext[:, HEAD_DIM_ATT:]
            for g in range(GQA):
                o_ref[0, sb * BLOCK:(sb + 1) * BLOCK,
                      (h * GQA + g) * HEAD_DIM_ATT:(h * GQA + g + 1) * HEAD_DIM_ATT] = (
                    o[g * BLOCK:(g + 1) * BLOCK].astype(o_ref.dtype))


def window_attention(proj, band_bias, sink):
    b, s, _ = proj.shape
    tq = min(ATT_Q_TILE, s)
    nblk = s // BLOCK
    r = tq // BLOCK
    assert r >= 2 and s % tq == 0
    k_col = ATT_WIDTH // KV_WIDTH
    v_col = k_col + 1

    def main(col):
        return pl.BlockSpec((1, tq, KV_WIDTH), lambda bi, i: (bi, i, col))

    def prev(col):
        return pl.BlockSpec((1, BLOCK, KV_WIDTH), lambda bi, i: (bi, jnp.maximum(i * r - 1, 0), col))

    def nxt(col):
        return pl.BlockSpec((1, BLOCK, KV_WIDTH),
                            lambda bi, i: (bi, jnp.minimum((i + 1) * r, nblk - 1), col))

    sink_rows = jnp.repeat(sink.astype(F32) * LOG2_E, BLOCK).reshape(N_KV_HEADS, GQA * BLOCK)
    return pl.pallas_call(
        functools.partial(_attn_kernel, tq=tq),
        grid=(b, s // tq),
        in_specs=[
            pl.BlockSpec((1, tq, ATT_WIDTH), lambda bi, i: (bi, i, 0)),
            prev(k_col), main(k_col), nxt(k_col),
            prev(v_col), main(v_col), nxt(v_col),
            pl.BlockSpec((N_KV_HEADS, 4, BLOCK, GQA * BLOCK), lambda bi, i: (0, 0, 0, 0)),
            pl.BlockSpec((N_KV_HEADS, GQA * BLOCK), lambda bi, i: (0, 0)),
        ],
        out_specs=pl.BlockSpec((1, tq, ATT_WIDTH), lambda bi, i: (bi, i, 0)),
        out_shape=jax.ShapeDtypeStruct((b, s, ATT_WIDTH), BF16),
        scratch_shapes=[pltpu.VMEM((tq + 2 * BLOCK, KV_WIDTH), BF16),
                        pltpu.VMEM((tq + 2 * BLOCK, KV_WIDTH), BF16)],
        compiler_params=_params("parallel", "arbitrary"),
        name="window_attention",
    )(proj, proj, proj, proj, proj, proj, proj, band_bias, sink_rows)


def _rotate(x, cos, sin):
    half = RET_HEAD_DIM // 2
    x1 = x[:, :half]
    x2 = x[:, half:]
    return jnp.concatenate([x1 * cos - x2 * sin, x1 * sin + x2 * cos], axis=-1)


def _ret_kernel(ld_ref, q_ref, k_ref, v_ref, g_ref, cos_ref, sin_ref, o_ref,
                oacc, qrot, kbwd, kv_scr, rprev, rstate, *, nseg, seg_len):
    h = pl.program_id(1)
    s = pl.program_id(2)
    cs = RET_CHUNK
    nchunk = seg_len // cs
    ld_f = ld_ref[0, h]
    ld_b = ld_ref[1, h]
    ii = lax.broadcasted_iota(jnp.int32, (cs, 1), 0).astype(F32)
    jj = lax.broadcasted_iota(jnp.int32, (1, cs), 1).astype(F32)
    k_scale = RET_HEAD_DIM ** -0.5
    nt = (((1,), (1,)), ((), ()))
    tn = (((0,), (0,)), ((), ()))
    chunk_len = jnp.full((1, 1), cs, F32)

    @pl.when(s == 0)
    def _():
        rstate[...] = jnp.zeros_like(rstate)

    @pl.when(s < nseg)
    def _():
        base = s * seg_len
        diff = ii - jj
        dmat = k_scale * jnp.where(diff >= 0, jnp.exp(ld_f * jnp.maximum(diff, 0.0)),
                                   jnp.exp(ld_b * jnp.maximum(-diff, 0.0)))
        q_dec = jnp.exp(ld_f * (ii + 1.0))
        k_dec_f = k_scale * jnp.exp(ld_f * (cs - 1.0 - ii))
        k_dec_b = k_scale * jnp.exp(ld_b * ii)
        g_chunk = jnp.exp(chunk_len * ld_f)

        def products(c, carry):
            r0 = pl.multiple_of(c * cs, cs)
            rows = pl.ds(pl.multiple_of(base + r0, cs), cs)
            cos = cos_ref[pl.ds(r0, cs), :]
            sin = sin_ref[pl.ds(r0, cs), :]
            q = _rotate(q_ref[0, pl.ds(r0, cs), :].astype(F32), cos, sin).astype(BF16)
            k = _rotate(k_ref[0, pl.ds(r0, cs), :].astype(F32), cos, sin)
            v = v_ref[0, pl.ds(r0, cs), :]
            qrot[rows, :] = q
            kbwd[rows, :] = (k * k_dec_b).astype(BF16)
            a = lax.dot_general(q, k.astype(BF16), nt, preferred_element_type=F32) * dmat
            oacc[rows, :] = jnp.dot(a.astype(BF16), v, preferred_element_type=F32)
            kv_scr[c] = lax.dot_general((k * k_dec_f).astype(BF16), v, tn, preferred_element_type=F32)
            return carry

        for c_static in range(nchunk):
            products(c_static, 0)

        def scan(c, carry):
            r = rstate[0]
            rprev[c] = r.astype(BF16)
            rstate[0] = g_chunk * r + kv_scr[c]
            return carry

        for c_static in range(nchunk):
            scan(c_static, 0)

        def cross(c, carry):
            rows = pl.ds(pl.multiple_of(base + c * cs, cs), cs)
            oacc[rows, :] += q_dec * jnp.dot(qrot[rows, :], rprev[c], preferred_element_type=F32)
            return carry

        for c_static in range(nchunk):
            cross(c_static, 0)

    @pl.when(s >= nseg)
    def _():
        base = (2 * nseg - 1 - s) * seg_len
        q_dec = jnp.exp(ld_b * (cs - ii))
        g_chunk = jnp.exp(chunk_len * ld_b)

        def products(c, carry):
            r0 = pl.multiple_of(c * cs, cs)
            rows = pl.ds(pl.multiple_of(base + r0, cs), cs)
            kv_scr[c] = lax.dot_general(kbwd[rows, :], v_ref[0, pl.ds(r0, cs), :], tn,
                                        preferred_element_type=F32)
            return carry

        for c_static in range(nchunk):
            products(c_static, 0)

        def scan(t, carry):
            c = nchunk - 1 - t
            r = rstate[1]
            rprev[c] = r.astype(BF16)
            rstate[1] = g_chunk * r + kv_scr[c]
            return carry

        for c_static in range(nchunk):
            scan(c_static, 0)

        def finish(c, carry):
            r0 = pl.multiple_of(c * cs, cs)
            rows = pl.ds(pl.multiple_of(base + r0, cs), cs)
            o = oacc[rows, :] + q_dec * jnp.dot(qrot[rows, :], rprev[c], preferred_element_type=F32)
            mu = jnp.mean(o, axis=-1, keepdims=True)
            oc = o - mu
            var = jnp.mean(oc * oc, axis=-1, keepdims=True)
            gate = g_ref[0, pl.ds(r0, cs), :].astype(F32)
            silu = gate * (1.0 / (1.0 + jnp.exp(-gate)))
            o_ref[0, pl.ds(r0, cs), :] = (oc * lax.rsqrt(var + EPS) * silu).astype(o_ref.dtype)
            return carry

        for c_static in range(nchunk):
            finish(c_static, 0)


def _rotary_tables(seq_len):
    half = RET_HEAD_DIM // 2
    inv = 1.0 / (10000.0 ** jnp.linspace(0.0, 1.0, half, dtype=F32))
    ang = jnp.arange(seq_len, dtype=F32)[:, None] * inv[None, :]
    return jnp.cos(ang), jnp.sin(ang)


def retention(proj, log_decay, cos, sin):
    b, s, _ = proj.shape
    seg_len = min(RET_SEG, s)
    nseg = s // seg_len
    nchunk = seg_len // RET_CHUNK
    base = (ATT_WIDTH + 2 * KV_WIDTH) // RET_HEAD_DIM

    def seg_both(t):
        return jnp.where(t < nseg, t, 2 * nseg - 1 - t)

    def seg_fwd_only(t):
        return jnp.minimum(t, nseg - 1)

    def seg_bwd_only(t):
        return jnp.where(t < nseg, nseg - 1, 2 * nseg - 1 - t)

    def head_block(which, seg):
        off = base + which * N_RET_HEADS
        return pl.BlockSpec((1, seg_len, RET_HEAD_DIM), lambda bi, h, t: (bi, seg(t), off + h))

    table = pl.BlockSpec((seg_len, RET_HEAD_DIM // 2), lambda bi, h, t: (seg_fwd_only(t), 0))
    return pl.pallas_call(
        functools.partial(_ret_kernel, nseg=nseg, seg_len=seg_len),
        grid=(b, N_RET_HEADS, 2 * nseg),
        in_specs=[pl.BlockSpec(memory_space=pltpu.SMEM),
                  head_block(0, seg_fwd_only), head_block(1, seg_fwd_only),
                  head_block(2, seg_both), head_block(3, seg_bwd_only), table, table],
        out_specs=pl.BlockSpec((1, seg_len, RET_HEAD_DIM),
                               lambda bi, h, t: (bi, seg_bwd_only(t), h)),
        out_shape=jax.ShapeDtypeStruct((b, s, RET_WIDTH), BF16),
        scratch_shapes=[pltpu.VMEM((s, RET_HEAD_DIM), F32),
                        pltpu.VMEM((s, RET_HEAD_DIM), BF16),
                        pltpu.VMEM((s, RET_HEAD_DIM), BF16),
                        pltpu.VMEM((nchunk, RET_HEAD_DIM, RET_HEAD_DIM), F32),
                        pltpu.VMEM((nchunk, RET_HEAD_DIM, RET_HEAD_DIM), BF16),
                        pltpu.VMEM((2, RET_HEAD_DIM, RET_HEAD_DIM), F32)],
        compiler_params=_params("parallel", "arbitrary", "arbitrary"),
        name="retention",
    )(log_decay, proj, proj, proj, proj, cos, sin)


def _outproj_kernel(att_ref, ret_ref, x_ref, wa_ref, wr_ref, g_ref, wrt_ref, xo_ref, hn_ref, aff_ref, ybuf):
    i = pl.program_id(0)
    cur = lax.rem(i, 2)

    @pl.when(i == 0)
    def _():
        ybuf[1] = jnp.zeros(ybuf.shape[1:], F32)

    y_prev = ybuf[1 - cur]
    var = jnp.mean(y_prev * y_prev, axis=-1, keepdims=True)
    hn = (y_prev * lax.rsqrt(var + EPS) * g_ref[...]).astype(BF16)
    hn_ref[...] = hn
    logits = jnp.dot(hn, wrt_ref[...], preferred_element_type=F32)
    lane = lax.broadcasted_iota(jnp.int32, (1, LANES), 1)
    logits = jnp.where(lane < N_EXPERTS, logits, NEG_INF)
    e = jnp.exp(logits - jnp.max(logits, axis=-1, keepdims=True))
    aff = e / jnp.sum(e, axis=-1, keepdims=True)
    aff_ref[...] = jnp.transpose(aff)[:N_EXPERTS]

    y = x_ref[...] + jnp.dot(att_ref[...], wa_ref[...], preferred_element_type=F32)
    y = y + jnp.dot(ret_ref[...], wr_ref[...], preferred_element_type=F32)
    xo_ref[...] = y
    ybuf[cur] = y


def outproj_norm_router(att, ret, x2d, w_out_bf16, g, w_router_pad):
    n, d = x2d.shape
    tm = min(ROW_TILE_OUTPROJ, n)
    nt = n // tm
    row = lambda i: (jnp.minimum(i, nt - 1), 0)
    prev_row = lambda i: (jnp.maximum(i - 1, 0), 0)
    return pl.pallas_call(
        _outproj_kernel,
        grid=(nt + 1,),
        in_specs=[
            pl.BlockSpec((tm, ATT_WIDTH), row),
            pl.BlockSpec((tm, RET_WIDTH), row),
            pl.BlockSpec((tm, d), row),
            pl.BlockSpec((ATT_WIDTH, d), lambda i: (0, 0)),
            pl.BlockSpec((RET_WIDTH, d), lambda i: (1, 0)),
            pl.BlockSpec((1, d), lambda i: (0, 0)),
            pl.BlockSpec((d, LANES), lambda i: (0, 0)),
        ],
        out_specs=[pl.BlockSpec((tm, d), row), pl.BlockSpec((tm, d), prev_row),
                   pl.BlockSpec((N_EXPERTS, tm), lambda i: (0, jnp.maximum(i - 1, 0)))],
        out_shape=[jax.ShapeDtypeStruct((n, d), F32), jax.ShapeDtypeStruct((n, d), BF16),
                   jax.ShapeDtypeStruct((N_EXPERTS, n), F32)],
        scratch_shapes=[pltpu.VMEM((2, tm, d), F32)],
        compiler_params=_params("arbitrary"),
        name="outproj_norm_router",
    )(att, ret, x2d, w_out_bf16, w_out_bf16, g.reshape(1, d), w_router_pad)


def _moe_kernel(xe_ref, wg_ref, wu_ref, wd_ref, gate_ref, o_ref):
    x = xe_ref[0]
    hg = jnp.dot(x, wg_ref[0], preferred_element_type=F32)
    hu = jnp.dot(x, wu_ref[0], preferred_element_type=F32)
    h = hg * (1.0 / (1.0 + jnp.exp(-hg))) * hu
    y = jnp.dot(h.astype(BF16), wd_ref[0], preferred_element_type=F32)
    gate = jnp.transpose(gate_ref[0, 0])[:, 0:1]
    o_ref[0] = (y * gate).astype(o_ref.dtype)


def expert_ffn(xe, gates, wg, wu, wd):
    e, cap, d = xe.shape
    ff = wg.shape[-1]
    tc = min(MOE_SLOT_TILE, cap)
    return pl.pallas_call(
        _moe_kernel,
        grid=(e, cap // tc),
        in_specs=[
            pl.BlockSpec((1, tc, d), lambda ei, t: (ei, t, 0)),
            pl.BlockSpec((1, d, ff), lambda ei, t: (ei, 0, 0)),
            pl.BlockSpec((1, d, ff), lambda ei, t: (ei, 0, 0)),
            pl.BlockSpec((1, ff, d), lambda ei, t: (ei, 0, 0)),
            pl.BlockSpec((1, 1, GATE_ROWS, tc), lambda ei, t: (ei, t, 0, 0)),
        ],
        out_specs=pl.BlockSpec((1, tc, d), lambda ei, t: (ei, t, 0)),
        out_shape=jax.ShapeDtypeStruct((e, cap, d), BF16),
        compiler_params=_params("parallel", "arbitrary"),
        name="expert_ffn",
    )(xe, wg, wu, wd, jnp.broadcast_to(gates.reshape(e, cap // tc, 1, tc), (e, cap // tc, GATE_ROWS, tc)))


def _combine_kernel(lo_ref, hi_ref, x_ref, tok_ref, g_ref, ye_hbm, o_ref, buf, sem, *, tile, blk, apply_norm):
    i = pl.program_id(0)
    lo = lo_ref[i]
    hi = hi_ref[i]
    t0 = i * tile
    ahead = COMBINE_BUFFERS - 1

    def fetch(c):
        slot = lax.rem(c, COMBINE_BUFFERS)
        return pltpu.make_async_copy(ye_hbm.at[pl.ds(pl.multiple_of(c * blk, blk), blk)],
                                     buf.at[slot], sem.at[slot])

    def start_first_blocks(first, end):
        for k in range(ahead):
            @pl.when(first + k < end)
            def _():
                fetch(first + k).start()

    @pl.when(i == 0)
    def _():
        start_first_blocks(lo, hi)

    o_ref[...] = x_ref[...]
    row = lax.broadcasted_iota(jnp.int32, (tile, blk), 0)

    def body(c, carry):
        fetch(c).wait()

        @pl.when(c + ahead < hi)
        def _():
            fetch(c + ahead).start()

        rel = tok_ref[pl.ds(c, 1), :] - t0
        sel = jnp.where(row == rel, 1.0, 0.0).astype(BF16)
        o_ref[...] += jnp.dot(sel, buf[lax.rem(c, COMBINE_BUFFERS)], preferred_element_type=F32)
        return carry

    lax.fori_loop(lo, hi, body, 0)

    @pl.when(i + 1 < pl.num_programs(0))
    def _():
        start_first_blocks(lo_ref[i + 1], hi_ref[i + 1])

    if apply_norm:
        y = o_ref[...]
        var = jnp.mean(y * y, axis=-1, keepdims=True)
        o_ref[...] = y * lax.rsqrt(var + EPS) * g_ref[...]


def combine(x2d, ye_sorted, tok_sorted, final_g):
    n, d = x2d.shape
    rows = ye_sorted.shape[0]
    tile = min(COMBINE_TOKEN_TILE, n)
    blk = min(COMBINE_ROW_BLOCK, rows)
    bounds = jnp.searchsorted(tok_sorted, jnp.arange(0, n + 1, tile, dtype=jnp.int32),
                              method="compare_all").astype(jnp.int32)
    start, end = bounds[:-1], bounds[1:]
    lo = start // blk
    hi = jnp.where(end > start, (end + blk - 1) // blk, lo)
    apply_norm = final_g is not None
    g = (final_g if apply_norm else jnp.ones((d,), F32)).reshape(1, d)
    return pl.pallas_call(
        functools.partial(_combine_kernel, tile=tile, blk=blk, apply_norm=apply_norm),
        grid=(n // tile,),
        in_specs=[pl.BlockSpec(memory_space=pltpu.SMEM), pl.BlockSpec(memory_space=pltpu.SMEM),
                  pl.BlockSpec((tile, d), lambda i: (i, 0)),
                  pl.BlockSpec((rows // blk, blk), lambda i: (0, 0)),
                  pl.BlockSpec((1, d), lambda i: (0, 0)),
                  pl.BlockSpec(memory_space=pl.ANY)],
        out_specs=pl.BlockSpec((tile, d), lambda i: (i, 0)),
        out_shape=jax.ShapeDtypeStruct((n, d), F32),
        scratch_shapes=[pltpu.VMEM((COMBINE_BUFFERS, blk, d), BF16),
                        pltpu.SemaphoreType.DMA((COMBINE_BUFFERS,))],
        compiler_params=_params("arbitrary"),
        name="combine",
    )(lo, hi, x2d, tok_sorted.reshape(rows // blk, blk), g, ye_sorted)


def _layer(x2d, bsz, seq, lw, moe_stacks, moe_cache, band_bias, cos, sin, final_g):
    n = bsz * seq
    layer = lw["layer"]
    if layer in moe_cache:
        proj, _ = norm_inproj(x2d, lw["norm_mix_g"], lw["w_in"])
    elif all(side_cast_fits(n, st) for st in moe_stacks):
        proj, moe_cache[layer] = norm_inproj(x2d, lw["norm_mix_g"], lw["w_in"], moe_stacks, layer)
    else:
        proj, _ = norm_inproj(x2d, lw["norm_mix_g"], lw["w_in"])
        moe_cache[layer] = [layer_weight_bf16(st, layer) for st in moe_stacks]
    w_gate, w_up, w_down = moe_cache[layer]
    proj = proj.reshape(bsz, seq, IN_WIDTH)
    att = window_attention(proj, band_bias, lw["attn_sink"]).reshape(n, ATT_WIDTH)
    ret = retention(proj, lw["log_decay"], cos, sin).reshape(n, RET_WIDTH)
    x2d, hn, aff = outproj_norm_router(att, ret, x2d, lw["w_out"], lw["norm_ffn_g"], lw["w_router"])
    cap = CAPACITY_FACTOR * n // N_EXPERTS
    gates, idx = lax.top_k(aff, cap)
    ye = expert_ffn(hn[idx], gates, w_gate, w_up, w_down)
    slots = N_EXPERTS * cap
    tok_sorted, perm = lax.sort_key_val(idx.reshape(slots).astype(jnp.int32),
                                        jnp.arange(slots, dtype=jnp.int32))
    return combine(x2d, ye.reshape(slots, D_MODEL)[perm], tok_sorted, final_g)


def kernel(x_prompt, x_sample, norm_mix_g, w_in, attn_sink, rel_bias, ret_decay_fwd, ret_decay_bwd,
           w_out, norm_ffn_g, w_router, w_gate, w_up, w_down, norm_final_g):
    depth = w_in.shape[0]
    band_bias = _band_bias(rel_bias)
    layers = []
    for l in range(depth):
        layers.append({
            "layer": l,
            "norm_mix_g": norm_mix_g[l],
            "w_in": layer_weight_bf16(w_in, l),
            "attn_sink": attn_sink[l],
            "log_decay": jnp.stack([jax.nn.log_sigmoid(ret_decay_fwd[l].astype(F32)),
                                    jax.nn.log_sigmoid(ret_decay_bwd[l].astype(F32))]),
            "w_out": layer_weight_bf16(w_out, l),
            "norm_ffn_g": norm_ffn_g[l],
            "w_router": jnp.pad(w_router[l].astype(BF16), ((0, 0), (0, LANES - N_EXPERTS))),
        })
    moe_stacks = (w_gate, w_up, w_down)
    moe_cache = {}

    def trunk(x):
        bsz, seq, d = x.shape
        cos, sin = _rotary_tables(seq)
        x2d = x.reshape(bsz * seq, d)
        for l, lw in enumerate(layers):
            x2d = _layer(x2d, bsz, seq, lw, moe_stacks, moe_cache, band_bias, cos, sin,
                         norm_final_g if l == depth - 1 else None)
        return x2d.reshape(bsz, seq, d)

    return (trunk(x_prompt), trunk(x_sample))
```

```python
import functools
import math

import jax
import jax.numpy as jnp
from jax import lax
from jax.experimental import pallas as pl
from jax.experimental.pallas import tpu as pltpu

D_MODEL = 2048
HEAD_DIM_ATT = 128
N_Q_HEADS = 8
N_KV_HEADS = 2
GQA = N_Q_HEADS // N_KV_HEADS
ATT_WIDTH = N_Q_HEADS * HEAD_DIM_ATT
KV_WIDTH = N_KV_HEADS * HEAD_DIM_ATT
WINDOW = 128
BLOCK = 128
N_BUCKETS = 32
MAX_DISTANCE = 128
N_RET_HEADS = 4
RET_HEAD_DIM = 256
RET_WIDTH = N_RET_HEADS * RET_HEAD_DIM
RET_CHUNK = 256
IN_WIDTH = ATT_WIDTH + 2 * KV_WIDTH + 4 * RET_WIDTH
N_EXPERTS = 16
CAPACITY_FACTOR = 2
EXPERT_FF = D_MODEL // 2
EPS = 1e-6
NEG_INF = -1e30
LANES = 128
LOG2_E = math.log2(math.e)
SINK_ROWS = 16

VMEM_LIMIT_BYTES = 56 * 1024 * 1024

INPROJ_ROW_TILE = 512
INPROJ_ROW_TILE_WITH_CAST = 256
ROW_TILE_OUTPROJ = 512
ATT_Q_TILE = 512
RET_SEG = 2048
MOE_SLOT_TILE = 512
GATE_ROWS = 8
COMBINE_TOKEN_TILE = 256
CAST_BLOCK_ELEMS = 1024 * 1024
SIDE_CAST_BLOCK_ELEMS = 512 * 1024
COMBINE_ROW_BLOCK = 256
COMBINE_BUFFERS = 4

BF16 = jnp.bfloat16
F32 = jnp.float32


def _params(*sem):
    return pltpu.CompilerParams(dimension_semantics=sem, vmem_limit_bytes=VMEM_LIMIT_BYTES)


def _cast_kernel(w_ref, o_ref):
    o_ref[...] = w_ref[0].astype(o_ref.dtype)


def layer_weight_bf16(w_stacked, layer):
    shape = w_stacked.shape[1:]
    cols = shape[-1]
    rows = math.prod(shape[:-1])
    tr = rows
    while tr * cols > CAST_BLOCK_ELEMS and tr % 32 == 0:
        tr //= 2
    out = pl.pallas_call(
        _cast_kernel,
        grid=(rows // tr,),
        in_specs=[pl.BlockSpec((1, tr, cols), lambda i: (layer, i, 0))],
        out_specs=pl.BlockSpec((tr, cols), lambda i: (i, 0)),
        out_shape=jax.ShapeDtypeStruct((rows, cols), BF16),
        compiler_params=_params("parallel"),
        name="weight_cast",
    )(w_stacked.reshape(w_stacked.shape[0], rows, cols))
    return out.reshape(shape)


def _norm_inproj_kernel(x_ref, g_ref, w_ref, *rest, n_cast):
    cast_in = rest[:n_cast]
    o_ref = rest[n_cast]
    cast_out = rest[n_cast + 1:2 * n_cast + 1]
    hbuf = rest[2 * n_cast + 1]
    i = pl.program_id(0)

    @pl.when(i == 0)
    def _():
        hbuf[1] = jnp.zeros(hbuf.shape[1:], BF16)

    def step(cur):
        o_ref[...] = jnp.dot(hbuf[1 - cur], w_ref[...], preferred_element_type=F32).astype(o_ref.dtype)
        x = x_ref[...]
        var = jnp.mean(x * x, axis=-1, keepdims=True)
        hbuf[cur] = (x * lax.rsqrt(var + EPS) * g_ref[...]).astype(BF16)
        for src, dst in zip(cast_in, cast_out):
            dst[...] = src[0].astype(dst.dtype)

    @pl.when(lax.rem(i, 2) == 0)
    def _():
        step(0)

    @pl.when(lax.rem(i, 2) == 1)
    def _():
        step(1)


def _inproj_row_tile(n, casting):
    return min(INPROJ_ROW_TILE_WITH_CAST if casting else INPROJ_ROW_TILE, n)


def side_cast_fits(n, stacked):
    steps = n // _inproj_row_tile(n, True)
    rows = math.prod(stacked.shape[1:-1])
    per_step = rows // steps
    return (rows % steps == 0 and per_step % 16 == 0
            and per_step * stacked.shape[-1] <= SIDE_CAST_BLOCK_ELEMS)


def norm_inproj(x2d, g, w_bf16, cast_stacks=(), cast_layer=0):
    n, d = x2d.shape
    width = w_bf16.shape[1]
    tm = _inproj_row_tile(n, bool(cast_stacks))
    nt = n // tm
    cur_row = lambda i: (jnp.minimum(i, nt - 1), 0)
    prev_row = lambda i: (jnp.maximum(i - 1, 0), 0)
    in_specs = [
        pl.BlockSpec((tm, d), cur_row),
        pl.BlockSpec((1, d), lambda i: (0, 0)),
        pl.BlockSpec((d, width), lambda i: (0, 0), pipeline_mode=pl.Buffered(1)),
    ]
    out_specs = [pl.BlockSpec((tm, width), prev_row)]
    out_shape = [jax.ShapeDtypeStruct((n, width), BF16)]
    cast_args = []
    for st in cast_stacks:
        cols = st.shape[-1]
        rows = math.prod(st.shape[1:-1])
        per_step = rows // nt
        in_specs.append(pl.BlockSpec((1, per_step, cols), lambda i: (cast_layer, jnp.minimum(i, nt - 1), 0)))
        out_specs.append(pl.BlockSpec((per_step, cols), lambda i: (jnp.minimum(i, nt - 1), 0)))
        out_shape.append(jax.ShapeDtypeStruct((rows, cols), BF16))
        cast_args.append(st.reshape(st.shape[0], rows, cols))
    outs = pl.pallas_call(
        functools.partial(_norm_inproj_kernel, n_cast=len(cast_stacks)),
        grid=(nt + 1,),
        in_specs=in_specs,
        out_specs=out_specs,
        out_shape=out_shape,
        scratch_shapes=[pltpu.VMEM((2, tm, d), BF16)],
        compiler_params=_params("arbitrary"),
        name="norm_inproj",
    )(x2d, g.reshape(1, d), w_bf16, *cast_args)
    return outs[0], [o.reshape(st.shape[1:]) for o, st in zip(outs[1:], cast_stacks)]


def _t5_bucket(rel):
    half = N_BUCKETS // 2
    max_exact = half // 2
    bucket = jnp.where(rel > 0, half, 0)
    n = jnp.abs(rel)
    nf = jnp.maximum(n, 1).astype(F32)
    large = max_exact + (jnp.log(nf / max_exact) / math.log(MAX_DISTANCE / max_exact)
                         * (half - max_exact)).astype(jnp.int32)
    large = jnp.minimum(large, half - 1)
    return bucket + jnp.where(n < max_exact, n, large)


def _band_bias(rel_bias):
    qi = jnp.arange(BLOCK)[:, None]
    kj = jnp.arange(3 * BLOCK)[None, :]
    rel = kj - BLOCK - qi
    onehot = jax.nn.one_hot(_t5_bucket(rel), N_BUCKETS, dtype=F32)
    bias = jnp.einsum('qkb,bh->qkh', onehot, rel_bias.astype(F32),
                      precision=lax.Precision.HIGHEST)
    bias = jnp.where((jnp.abs(rel) <= WINDOW)[:, :, None], bias, NEG_INF)
    bias = bias.reshape(BLOCK, 3, BLOCK, N_KV_HEADS, GQA).transpose(3, 1, 2, 4, 0)
    bias = bias.reshape(N_KV_HEADS, 3, BLOCK, GQA * BLOCK)
    return jnp.concatenate([bias, jnp.full_like(bias[:, :1], NEG_INF)], axis=1) * LOG2_E


def _attn_kernel(q_ref, kp_ref, km_ref, kn_ref, vp_ref, vm_ref, vn_ref, bias_ref, sink_ref,
                 o_ref, k_scr, v_scr, *, tq):
    i = pl.program_id(1)
    last = pl.num_programs(1) - 1
    k_scr[0:BLOCK] = kp_ref[0]
    k_scr[BLOCK:BLOCK + tq] = km_ref[0]
    k_scr[BLOCK + tq:] = kn_ref[0]
    v_scr[0:BLOCK] = vp_ref[0]
    v_scr[BLOCK:BLOCK + tq] = vm_ref[0]
    v_scr[BLOCK + tq:] = vn_ref[0]
    scale = HEAD_DIM_ATT ** -0.5 * LOG2_E
    nsb = tq // BLOCK
    nt = (((1,), (1,)), ((), ()))
    tn = (((0,), (0,)), ((), ()))
    ones = jnp.ones((3 * BLOCK, HEAD_DIM_ATT), BF16)
    r_id = lax.broadcasted_iota(jnp.int32, (SINK_ROWS, 2 * HEAD_DIM_ATT), 0)
    c_id = lax.broadcasted_iota(jnp.int32, (SINK_ROWS, 2 * HEAD_DIM_ATT), 1)
    sink_rows = jnp.where((r_id == 0) & (c_id >= HEAD_DIM_ATT), 1.0, 0.0).astype(BF16)
    for sb in range(nsb):
        for h in range(N_KV_HEADS):
            c0 = h * HEAD_DIM_ATT
            qs = jnp.concatenate(
                [q_ref[0, sb * BLOCK:(sb + 1) * BLOCK,
                       (h * GQA + g) * HEAD_DIM_ATT:(h * GQA + g + 1) * HEAD_DIM_ATT]
                 for g in range(GQA)], axis=0)
            parts = []
            for c in range(3):
                kc = k_scr[(sb + c) * BLOCK:(sb + c + 1) * BLOCK, c0:c0 + HEAD_DIM_ATT]
                if c == 0 and sb == 0:
                    slot = jnp.where(i == 0, 3, 0)
                elif c == 2 and sb == nsb - 1:
                    slot = jnp.where(i == last, 3, 2)
                else:
                    slot = c
                s_c = lax.dot_general(kc, qs, nt, preferred_element_type=F32)
                parts.append(s_c * scale + bias_ref[h, slot])
            s = jnp.concatenate(parts, axis=0)
            sink = sink_ref[h:h + 1, :]
            m = jnp.maximum(jnp.max(s, axis=0, keepdims=True), sink)
            p = jnp.exp2(s - m).astype(BF16)
            p_sink = jnp.broadcast_to(jnp.exp2(sink - m).astype(BF16), (SINK_ROWS, GQA * BLOCK))
            p_ext = jnp.concatenate([p, p_sink], axis=0)
            vb = v_scr[sb * BLOCK:(sb + 3) * BLOCK, c0:c0 + HEAD_DIM_ATT]
            v_ext = jnp.concatenate([jnp.concatenate([vb, ones], axis=1), sink_rows], axis=0)
            o_ext = lax.dot_general(p_ext, v_ext, tn, preferred_element_type=F32)
            o = o_ext[:, :HEAD_DIM_ATT] / o_ext[:, HEAD_DIM_ATT:]
            for g in range(GQA):
                o_ref[0, sb * BLOCK:(sb + 1) * BLOCK,
                      (h * GQA + g) * HEAD_DIM_ATT:(h * GQA + g + 1) * HEAD_DIM_ATT] = (
                    o[g * BLOCK:(g + 1) * BLOCK].astype(o_ref.dtype))


def window_attention(proj, band_bias, sink):
    b, s, _ = proj.shape
    tq = min(ATT_Q_TILE, s)
    nblk = s // BLOCK
    r = tq // BLOCK
    assert r >= 2 and s % tq == 0
    k_col = ATT_WIDTH // KV_WIDTH
    v_col = k_col + 1

    def main(col):
        return pl.BlockSpec((1, tq, KV_WIDTH), lambda bi, i: (bi, i, col))

    def prev(col):
        return pl.BlockSpec((1, BLOCK, KV_WIDTH), lambda bi, i: (bi, jnp.maximum(i * r - 1, 0), col))

    def nxt(col):
        return pl.BlockSpec((1, BLOCK, KV_WIDTH),
                            lambda bi, i: (bi, jnp.minimum((i + 1) * r, nblk - 1), col))

    sink_rows = jnp.repeat(sink.astype(F32) * LOG2_E, BLOCK).reshape(N_KV_HEADS, GQA * BLOCK)
    return pl.pallas_call(
        functools.partial(_attn_kernel, tq=tq),
        grid=(b, s // tq),
        in_specs=[
            pl.BlockSpec((1, tq, ATT_WIDTH), lambda bi, i: (bi, i, 0)),
            prev(k_col), main(k_col), nxt(k_col),
            prev(v_col), main(v_col), nxt(v_col),
            pl.BlockSpec((N_KV_HEADS, 4, BLOCK, GQA * BLOCK), lambda bi, i: (0, 0, 0, 0)),
            pl.BlockSpec((N_KV_HEADS, GQA * BLOCK), lambda bi, i: (0, 0)),
        ],
        out_specs=pl.BlockSpec((1, tq, ATT_WIDTH), lambda bi, i: (bi, i, 0)),
        out_shape=jax.ShapeDtypeStruct((b, s, ATT_WIDTH), BF16),
        scratch_shapes=[pltpu.VMEM((tq + 2 * BLOCK, KV_WIDTH), BF16),
                        pltpu.VMEM((tq + 2 * BLOCK, KV_WIDTH), BF16)],
        compiler_params=_params("parallel", "arbitrary"),
        name="window_attention",
    )(proj, proj, proj, proj, proj, proj, proj, band_bias, sink_rows)


def _rotate(x, cos, sin):
    half = RET_HEAD_DIM // 2
    x1 = x[:, :half]
    x2 = x[:, half:]
    return jnp.concatenate([x1 * cos - x2 * sin, x1 * sin + x2 * cos], axis=-1)


def _ret_kernel(ld_ref, q_ref, k_ref, v_ref, g_ref, cos_ref, sin_ref, o_ref,
                oacc, qrot, kbwd, kv_scr, rprev, rstate, *, nseg, seg_len):
    h = pl.program_id(1)
    s = pl.program_id(2)
    cs = RET_CHUNK
    nchunk = seg_len // cs
    ld_f = ld_ref[0, h]
    ld_b = ld_ref[1, h]
    ii = lax.broadcasted_iota(jnp.int32, (cs, 1), 0).astype(F32)
    jj = lax.broadcasted_iota(jnp.int32, (1, cs), 1).astype(F32)
    k_scale = RET_HEAD_DIM ** -0.5
    nt = (((1,), (1,)), ((), ()))
    tn = (((0,), (0,)), ((), ()))
    chunk_len = jnp.full((1, 1), cs, F32)

    @pl.when(s == 0)
    def _():
        rstate[...] = jnp.zeros_like(rstate)

    @pl.when(s < nseg)
    def _():
        base = s * seg_len
        diff = ii - jj
        dmat = k_scale * jnp.where(diff >= 0, jnp.exp(ld_f * jnp.maximum(diff, 0.0)),
                                   jnp.exp(ld_b * jnp.maximum(-diff, 0.0)))
        q_dec = jnp.exp(ld_f * (ii + 1.0))
        k_dec_f = k_scale * jnp.exp(ld_f * (cs - 1.0 - ii))
        k_dec_b = k_scale * jnp.exp(ld_b * ii)
        g_chunk = jnp.exp(chunk_len * ld_f)

        def products(c, carry):
            r0 = pl.multiple_of(c * cs, cs)
            rows = pl.ds(pl.multiple_of(base + r0, cs), cs)
            cos = cos_ref[pl.ds(r0, cs), :]
            sin = sin_ref[pl.ds(r0, cs), :]
            q = _rotate(q_ref[0, pl.ds(r0, cs), :].astype(F32), cos, sin).astype(BF16)
            k = _rotate(k_ref[0, pl.ds(r0, cs), :].astype(F32), cos, sin)
            v = v_ref[0, pl.ds(r0, cs), :]
            qrot[rows, :] = q
            kbwd[rows, :] = (k * k_dec_b).astype(BF16)
            a = lax.dot_general(q, k.astype(BF16), nt, preferred_element_type=F32) * dmat
            oacc[rows, :] = jnp.dot(a.astype(BF16), v, preferred_element_type=F32)
            kv_scr[c] = lax.dot_general((k * k_dec_f).astype(BF16), v, tn, preferred_element_type=F32)
            return carry

        for c_static in range(nchunk):
            products(c_static, 0)

        def scan(c, carry):
            r = rstate[0]
            rprev[c] = r.astype(BF16)
            rstate[0] = g_chunk * r + kv_scr[c]
            return carry

        for c_static in range(nchunk):
            scan(c_static, 0)

        def cross(c, carry):
            rows = pl.ds(pl.multiple_of(base + c * cs, cs), cs)
            oacc[rows, :] += q_dec * jnp.dot(qrot[rows, :], rprev[c], preferred_element_type=F32)
            return carry

        for c_static in range(nchunk):
            cross(c_static, 0)

    @pl.when(s >= nseg)
    def _():
        base = (2 * nseg - 1 - s) * seg_len
        q_dec = jnp.exp(ld_b * (cs - ii))
        g_chunk = jnp.exp(chunk_len * ld_b)

        def products(c, carry):
            r0 = pl.multiple_of(c * cs, cs)
            rows = pl.ds(pl.multiple_of(base + r0, cs), cs)
            kv_scr[c] = lax.dot_general(kbwd[rows, :], v_ref[0, pl.ds(r0, cs), :], tn,
                                        preferred_element_type=F32)
            return carry

        for c_static in range(nchunk):
            products(c_static, 0)

        def scan(t, carry):
            c = nchunk - 1 - t
            r = rstate[1]
            rprev[c] = r.astype(BF16)
            rstate[1] = g_chunk * r + kv_scr[c]
            return carry

        for c_static in range(nchunk):
            scan(c_static, 0)

        def finish(c, carry):
            r0 = pl.multiple_of(c * cs, cs)
            rows = pl.ds(pl.multiple_of(base + r0, cs), cs)
            o = oacc[rows, :] + q_dec * jnp.dot(qrot[rows, :], rprev[c], preferred_element_type=F32)
            mu = jnp.mean(o, axis=-1, keepdims=True)
            oc = o - mu
            var = jnp.mean(oc * oc, axis=-1, keepdims=True)
            gate = g_ref[0, pl.ds(r0, cs), :].astype(F32)
            silu = gate * (1.0 / (1.0 + jnp.exp(-gate)))
            o_ref[0, pl.ds(r0, cs), :] = (oc * lax.rsqrt(var + EPS) * silu).astype(o_ref.dtype)
            return carry

        for c_static in range(nchunk):
            finish(c_static, 0)


def _rotary_tables(seq_len):
    half = RET_HEAD_DIM // 2
    inv = 1.0 / (10000.0 ** jnp.linspace(0.0, 1.0, half, dtype=F32))
    ang = jnp.arange(seq_len, dtype=F32)[:, None] * inv[None, :]
    return jnp.cos(ang), jnp.sin(ang)


def retention(proj, log_decay, cos, sin):
    b, s, _ = proj.shape
    seg_len = min(RET_SEG, s)
    nseg = s // seg_len
    nchunk = seg_len // RET_CHUNK
    base = (ATT_WIDTH + 2 * KV_WIDTH) // RET_HEAD_DIM

    def seg_both(t):
        return jnp.where(t < nseg, t, 2 * nseg - 1 - t)

    def seg_fwd_only(t):
        return jnp.minimum(t, nseg - 1)

    def seg_bwd_only(t):
        return jnp.where(t < nseg, nseg - 1, 2 * nseg - 1 - t)

    def head_block(which, seg):
        off = base + which * N_RET_HEADS
        return pl.BlockSpec((1, seg_len, RET_HEAD_DIM), lambda bi, h, t: (bi, seg(t), off + h))

    table = pl.BlockSpec((seg_len, RET_HEAD_DIM // 2), lambda bi, h, t: (seg_fwd_only(t), 0))
    return pl.pallas_call(
        functools.partial(_ret_kernel, nseg=nseg, seg_len=seg_len),
        grid=(b, N_RET_HEADS, 2 * nseg),
        in_specs=[pl.BlockSpec(memory_space=pltpu.SMEM),
                  head_block(0, seg_fwd_only), head_block(1, seg_fwd_only),
                  head_block(2, seg_both), head_block(3, seg_bwd_only), table, table],
        out_specs=pl.BlockSpec((1, seg_len, RET_HEAD_DIM),
                               lambda bi, h, t: (bi, seg_bwd_only(t), h)),
        out_shape=jax.ShapeDtypeStruct((b, s, RET_WIDTH), BF16),
        scratch_shapes=[pltpu.VMEM((s, RET_HEAD_DIM), F32),
                        pltpu.VMEM((s, RET_HEAD_DIM), BF16),
                        pltpu.VMEM((s, RET_HEAD_DIM), BF16),
                        pltpu.VMEM((nchunk, RET_HEAD_DIM, RET_HEAD_DIM), F32),
                        pltpu.VMEM((nchunk, RET_HEAD_DIM, RET_HEAD_DIM), BF16),
                        pltpu.VMEM((2, RET_HEAD_DIM, RET_HEAD_DIM), F32)],
        compiler_params=_params("parallel", "arbitrary", "arbitrary"),
        name="retention",
    )(log_decay, proj, proj, proj, proj, cos, sin)


def _outproj_kernel(att_ref, ret_ref, x_ref, wa_ref, wr_ref, g_ref, wrt_ref, xo_ref, hn_ref, aff_ref, ybuf):
    i = pl.program_id(0)
    cur = lax.rem(i, 2)

    @pl.when(i == 0)
    def _():
        ybuf[1] = jnp.zeros(ybuf.shape[1:], F32)

    y_prev = ybuf[1 - cur]
    var = jnp.mean(y_prev * y_prev, axis=-1, keepdims=True)
    hn = (y_prev * lax.rsqrt(var + EPS) * g_ref[...]).astype(BF16)
    hn_ref[...] = hn
    logits = jnp.dot(hn, wrt_ref[...], preferred_element_type=F32)
    lane = lax.broadcasted_iota(jnp.int32, (1, LANES), 1)
    logits = jnp.where(lane < N_EXPERTS, logits, NEG_INF)
    e = jnp.exp(logits - jnp.max(logits, axis=-1, keepdims=True))
    aff = e / jnp.sum(e, axis=-1, keepdims=True)
    aff_ref[...] = jnp.transpose(aff)[:N_EXPERTS]

    y = x_ref[...] + jnp.dot(att_ref[...], wa_ref[...], preferred_element_type=F32)
    y = y + jnp.dot(ret_ref[...], wr_ref[...], preferred_element_type=F32)
    xo_ref[...] = y
    ybuf[cur] = y


def outproj_norm_router(att, ret, x2d, w_out_bf16, g, w_router_pad):
    n, d = x2d.shape
    tm = min(ROW_TILE_OUTPROJ, n)
    nt = n // tm
    row = lambda i: (jnp.minimum(i, nt - 1), 0)
    prev_row = lambda i: (jnp.maximum(i - 1, 0), 0)
    return pl.pallas_call(
        _outproj_kernel,
        grid=(nt + 1,),
        in_specs=[
            pl.BlockSpec((tm, ATT_WIDTH), row),
            pl.BlockSpec((tm, RET_WIDTH), row),
            pl.BlockSpec((tm, d), row),
            pl.BlockSpec((ATT_WIDTH, d), lambda i: (0, 0)),
            pl.BlockSpec((RET_WIDTH, d), lambda i: (1, 0)),
            pl.BlockSpec((1, d), lambda i: (0, 0)),
            pl.BlockSpec((d, LANES), lambda i: (0, 0)),
        ],
        out_specs=[pl.BlockSpec((tm, d), row), pl.BlockSpec((tm, d), prev_row),
                   pl.BlockSpec((N_EXPERTS, tm), lambda i: (0, jnp.maximum(i - 1, 0)))],
        out_shape=[jax.ShapeDtypeStruct((n, d), F32), jax.ShapeDtypeStruct((n, d), BF16),
                   jax.ShapeDtypeStruct((N_EXPERTS, n), F32)],
        scratch_shapes=[pltpu.VMEM((2, tm, d), F32)],
        compiler_params=_params("arbitrary"),
        name="outproj_norm_router",
    )(att, ret, x2d, w_out_bf16, w_out_bf16, g.reshape(1, d), w_router_pad)


def _moe_kernel(xe_ref, wg_ref, wu_ref, wd_ref, gate_ref, o_ref):
    x = xe_ref[0]
    hg = jnp.dot(x, wg_ref[0], preferred_element_type=F32)
    hu = jnp.dot(x, wu_ref[0], preferred_element_type=F32)
    h = hg * (1.0 / (1.0 + jnp.exp(-hg))) * hu
    y = jnp.dot(h.astype(BF16), wd_ref[0], preferred_element_type=F32)
    gate = jnp.transpose(gate_ref[0, 0])[:, 0:1]
    o_ref[0] = (y * gate).astype(o_ref.dtype)


def expert_ffn(xe, gates, wg, wu, wd):
    e, cap, d = xe.shape
    ff = wg.shape[-1]
    tc = min(MOE_SLOT_TILE, cap)
    return pl.pallas_call(
        _moe_kernel,
        grid=(e, cap // tc),
        in_specs=[
            pl.BlockSpec((1, tc, d), lambda ei, t: (ei, t, 0)),
            pl.BlockSpec((1, d, ff), lambda ei, t: (ei, 0, 0)),
            pl.BlockSpec((1, d, ff), lambda ei, t: (ei, 0, 0)),
            pl.BlockSpec((1, ff, d), lambda ei, t: (ei, 0, 0)),
            pl.BlockSpec((1, 1, GATE_ROWS, tc), lambda ei, t: (ei, t, 0, 0)),
        ],
        out_specs=pl.BlockSpec((1, tc, d), lambda ei, t: (ei, t, 0)),
        out_shape=jax.ShapeDtypeStruct((e, cap, d), BF16),
        compiler_params=_params("parallel", "arbitrary"),
        name="expert_ffn",
    )(xe, wg, wu, wd, jnp.broadcast_to(gates.reshape(e, cap // tc, 1, tc), (e, cap // tc, GATE_ROWS, tc)))


def _combine_kernel(lo_ref, hi_ref, x_ref, tok_ref, g_ref, ye_hbm, o_ref, buf, sem, *, tile, blk, apply_norm):
    i = pl.program_id(0)
    lo = lo_ref[i]
    hi = hi_ref[i]
    t0 = i * tile
    ahead = COMBINE_BUFFERS - 1

    def fetch(c):
        slot = lax.rem(c, COMBINE_BUFFERS)
        return pltpu.make_async_copy(ye_hbm.at[pl.ds(pl.multiple_of(c * blk, blk), blk)],
                                     buf.at[slot], sem.at[slot])

    def start_first_blocks(first, end):
        for k in range(ahead):
            @pl.when(first + k < end)
            def _():
                fetch(first + k).start()

    @pl.when(i == 0)
    def _():
        start_first_blocks(lo, hi)

    o_ref[...] = x_ref[...]
    row = lax.broadcasted_iota(jnp.int32, (tile, blk), 0)

    def body(c, carry):
        fetch(c).wait()

        @pl.when(c + ahead < hi)
        def _():
            fetch(c + ahead).start()

        rel = tok_ref[pl.ds(c, 1), :] - t0
        sel = jnp.where(row == rel, 1.0, 0.0).astype(BF16)
        o_ref[...] += jnp.dot(sel, buf[lax.rem(c, COMBINE_BUFFERS)], preferred_element_type=F32)
        return carry

    lax.fori_loop(lo, hi, body, 0)

    @pl.when(i + 1 < pl.num_programs(0))
    def _():
        start_first_blocks(lo_ref[i + 1], hi_ref[i + 1])

    if apply_norm:
        y = o_ref[...]
        var = jnp.mean(y * y, axis=-1, keepdims=True)
        o_ref[...] = y * lax.rsqrt(var + EPS) * g_ref[...]


def combine(x2d, ye_sorted, tok_sorted, final_g):
    n, d = x2d.shape
    rows = ye_sorted.shape[0]
    tile = min(COMBINE_TOKEN_TILE, n)
    blk = min(COMBINE_ROW_BLOCK, rows)
    bounds = jnp.searchsorted(tok_sorted, jnp.arange(0, n + 1, tile, dtype=jnp.int32),
                              method="compare_all").astype(jnp.int32)
    start, end = bounds[:-1], bounds[1:]
    lo = start // blk
    hi = jnp.where(end > start, (end + blk - 1) // blk, lo)
    apply_norm = final_g is not None
    g = (final_g if apply_norm else jnp.ones((d,), F32)).reshape(1, d)
    return pl.pallas_call(
        functools.partial(_combine_kernel, tile=tile, blk=blk, apply_norm=apply_norm),
        grid=(n // tile,),
        in_specs=[pl.BlockSpec(memory_space=pltpu.SMEM), pl.BlockSpec(memory_space=pltpu.SMEM),
                  pl.BlockSpec((tile, d), lambda i: (i, 0)),
                  pl.BlockSpec((rows // blk, blk), lambda i: (0, 0)),
                  pl.BlockSpec((1, d), lambda i: (0, 0)),
                  pl.BlockSpec(memory_space=pl.ANY)],
        out_specs=pl.BlockSpec((tile, d), lambda i: (i, 0)),
        out_shape=jax.ShapeDtypeStruct((n, d), F32),
        scratch_shapes=[pltpu.VMEM((COMBINE_BUFFERS, blk, d), BF16),
                        pltpu.SemaphoreType.DMA((COMBINE_BUFFERS,))],
        compiler_params=_params("arbitrary"),
        name="combine",
    )(lo, hi, x2d, tok_sorted.reshape(rows // blk, blk), g, ye_sorted)


def _top_k_tokens(aff, cap):
    keys = -lax.bitcast_convert_type(aff, jnp.int32)
    tokens = lax.broadcasted_iota(jnp.int32, aff.shape, 1)
    keys, tokens = lax.sort((keys, tokens), dimension=1, is_stable=True, num_keys=1)
    return lax.bitcast_convert_type(-keys[:, :cap], F32), tokens[:, :cap]


def _layer(x2d, bsz, seq, lw, moe_stacks, moe_cache, band_bias, cos, sin, final_g):
    n = bsz * seq
    layer = lw["layer"]
    if layer in moe_cache:
        proj, _ = norm_inproj(x2d, lw["norm_mix_g"], lw["w_in"])
    elif all(side_cast_fits(n, st) for st in moe_stacks):
        proj, moe_cache[layer] = norm_inproj(x2d, lw["norm_mix_g"], lw["w_in"], moe_stacks, layer)
    else:
        proj, _ = norm_inproj(x2d, lw["norm_mix_g"], lw["w_in"])
        moe_cache[layer] = [layer_weight_bf16(st, layer) for st in moe_stacks]
    w_gate, w_up, w_down = moe_cache[layer]
    proj = proj.reshape(bsz, seq, IN_WIDTH)
    att = window_attention(proj, band_bias, lw["attn_sink"]).reshape(n, ATT_WIDTH)
    ret = retention(proj, lw["log_decay"], cos, sin).reshape(n, RET_WIDTH)
    x2d, hn, aff = outproj_norm_router(att, ret, x2d, lw["w_out"], lw["norm_ffn_g"], lw["w_router"])
    cap = CAPACITY_FACTOR * n // N_EXPERTS
    gates, idx = _top_k_tokens(aff, cap)
    ye = expert_ffn(hn[idx], gates, w_gate, w_up, w_down)
    slots = N_EXPERTS * cap
    tok_sorted, perm = lax.sort_key_val(idx.reshape(slots).astype(jnp.int32),
                                        jnp.arange(slots, dtype=jnp.int32))
    return combine(x2d, ye.reshape(slots, D_MODEL)[perm], tok_sorted, final_g)


def kernel(x_prompt, x_sample, norm_mix_g, w_in, attn_sink, rel_bias, ret_decay_fwd, ret_decay_bwd,
           w_out, norm_ffn_g, w_router, w_gate, w_up, w_down, norm_final_g):
    depth = w_in.shape[0]
    band_bias = _band_bias(rel_bias)
    layers = []
    for l in range(depth):
        layers.append({
            "layer": l,
            "norm_mix_g": norm_mix_g[l],
            "w_in": layer_weight_bf16(w_in, l),
            "attn_sink": attn_sink[l],
            "log_decay": jnp.stack([jax.nn.log_sigmoid(ret_decay_fwd[l].astype(F32)),
                                    jax.nn.log_sigmoid(ret_decay_bwd[l].astype(F32))]),
            "w_out": layer_weight_bf16(w_out, l),
            "norm_ffn_g": norm_ffn_g[l],
            "w_router": jnp.pad(w_router[l].astype(BF16), ((0, 0), (0, LANES - N_EXPERTS))),
        })
    moe_stacks = (w_gate, w_up, w_down)
    moe_cache = {}

    def trunk(x):
        bsz, seq, d = x.shape
        cos, sin = _rotary_tables(seq)
        x2d = x.reshape(bsz * seq, d)
        for l, lw in enumerate(layers):
            x2d = _layer(x2d, bsz, seq, lw, moe_stacks, moe_cache, band_bias, cos, sin,
                         norm_final_g if l == depth - 1 else None)
        return x2d.reshape(bsz, seq, d)

    return (trunk(x_prompt), trunk(x_sample))
```

```python
import functools
import math

import jax
import jax.numpy as jnp
from jax import lax
from jax.experimental import pallas as pl
from jax.experimental.pallas import tpu as pltpu

D_MODEL = 2048
HEAD_DIM_ATT = 128
N_Q_HEADS = 8
N_KV_HEADS = 2
GQA = N_Q_HEADS // N_KV_HEADS
ATT_WIDTH = N_Q_HEADS * HEAD_DIM_ATT
KV_WIDTH = N_KV_HEADS * HEAD_DIM_ATT
WINDOW = 128
BLOCK = 128
N_BUCKETS = 32
MAX_DISTANCE = 128
N_RET_HEADS = 4
RET_HEAD_DIM = 256
RET_WIDTH = N_RET_HEADS * RET_HEAD_DIM
RET_CHUNK = 256
IN_WIDTH = ATT_WIDTH + 2 * KV_WIDTH + 4 * RET_WIDTH
N_EXPERTS = 16
CAPACITY_FACTOR = 2
EXPERT_FF = D_MODEL // 2
EPS = 1e-6
NEG_INF = -1e30
LANES = 128
LOG2_E = math.log2(math.e)
SINK_ROWS = 16

VMEM_LIMIT_BYTES = 56 * 1024 * 1024

INPROJ_ROW_TILE = 512
INPROJ_ROW_TILE_WITH_CAST = 256
ROW_TILE_OUTPROJ = 512
ATT_Q_TILE = 512
RET_SEG = 2048
MOE_SLOT_TILE = 512
GATE_ROWS = 8
COMBINE_TOKEN_TILE = 256
CAST_BLOCK_ELEMS = 1024 * 1024
SIDE_CAST_BLOCK_ELEMS = 512 * 1024
COMBINE_ROW_BLOCK = 256
COMBINE_BUFFERS = 4

BF16 = jnp.bfloat16
F32 = jnp.float32


def _params(*sem):
    return pltpu.CompilerParams(dimension_semantics=sem, vmem_limit_bytes=VMEM_LIMIT_BYTES)


def _cast_kernel(w_ref, o_ref):
    o_ref[...] = w_ref[0].astype(o_ref.dtype)


def layer_weight_bf16(w_stacked, layer):
    shape = w_stacked.shape[1:]
    cols = shape[-1]
    rows = math.prod(shape[:-1])
    tr = rows
    while tr * cols > CAST_BLOCK_ELEMS and tr % 32 == 0:
        tr //= 2
    out = pl.pallas_call(
        _cast_kernel,
        grid=(rows // tr,),
        in_specs=[pl.BlockSpec((1, tr, cols), lambda i: (layer, i, 0))],
        out_specs=pl.BlockSpec((tr, cols), lambda i: (i, 0)),
        out_shape=jax.ShapeDtypeStruct((rows, cols), BF16),
        compiler_params=_params("parallel"),
        name="weight_cast",
    )(w_stacked.reshape(w_stacked.shape[0], rows, cols))
    return out.reshape(shape)


def _norm_inproj_kernel(x_ref, g_ref, w_ref, *rest, n_cast):
    cast_in = rest[:n_cast]
    o_ref = rest[n_cast]
    cast_out = rest[n_cast + 1:2 * n_cast + 1]
    hbuf = rest[2 * n_cast + 1]
    i = pl.program_id(0)

    @pl.when(i == 0)
    def _():
        hbuf[1] = jnp.zeros(hbuf.shape[1:], BF16)

    def step(cur):
        o_ref[...] = jnp.dot(hbuf[1 - cur], w_ref[...], preferred_element_type=F32).astype(o_ref.dtype)
        x = x_ref[...]
        var = jnp.mean(x * x, axis=-1, keepdims=True)
        hbuf[cur] = (x * lax.rsqrt(var + EPS) * g_ref[...]).astype(BF16)
        for src, dst in zip(cast_in, cast_out):
            dst[...] = src[0].astype(dst.dtype)

    @pl.when(lax.rem(i, 2) == 0)
    def _():
        step(0)

    @pl.when(lax.rem(i, 2) == 1)
    def _():
        step(1)


def _inproj_row_tile(n, casting):
    return min(INPROJ_ROW_TILE_WITH_CAST if casting else INPROJ_ROW_TILE, n)


def side_cast_fits(n, stacked):
    steps = n // _inproj_row_tile(n, True)
    rows = math.prod(stacked.shape[1:-1])
    per_step = rows // steps
    return (rows % steps == 0 and per_step % 16 == 0
            and per_step * stacked.shape[-1] <= SIDE_CAST_BLOCK_ELEMS)


def norm_inproj(x2d, g, w_bf16, cast_stacks=(), cast_layer=0):
    n, d = x2d.shape
    width = w_bf16.shape[1]
    tm = _inproj_row_tile(n, bool(cast_stacks))
    nt = n // tm
    cur_row = lambda i: (jnp.minimum(i, nt - 1), 0)
    prev_row = lambda i: (jnp.maximum(i - 1, 0), 0)
    in_specs = [
        pl.BlockSpec((tm, d), cur_row),
        pl.BlockSpec((1, d), lambda i: (0, 0)),
        pl.BlockSpec((d, width), lambda i: (0, 0), pipeline_mode=pl.Buffered(1)),
    ]
    out_specs = [pl.BlockSpec((tm, width), prev_row)]
    out_shape = [jax.ShapeDtypeStruct((n, width), BF16)]
    cast_args = []
    for st in cast_stacks:
        cols = st.shape[-1]
        rows = math.prod(st.shape[1:-1])
        per_step = rows // nt
        in_specs.append(pl.BlockSpec((1, per_step, cols), lambda i: (cast_layer, jnp.minimum(i, nt - 1), 0)))
        out_specs.append(pl.BlockSpec((per_step, cols), lambda i: (jnp.minimum(i, nt - 1), 0)))
        out_shape.append(jax.ShapeDtypeStruct((rows, cols), BF16))
        cast_args.append(st.reshape(st.shape[0], rows, cols))
    outs = pl.pallas_call(
        functools.partial(_norm_inproj_kernel, n_cast=len(cast_stacks)),
        grid=(nt + 1,),
        in_specs=in_specs,
        out_specs=out_specs,
        out_shape=out_shape,
        scratch_shapes=[pltpu.VMEM((2, tm, d), BF16)],
        compiler_params=_params("arbitrary"),
        name="norm_inproj",
    )(x2d, g.reshape(1, d), w_bf16, *cast_args)
    return outs[0], [o.reshape(st.shape[1:]) for o, st in zip(outs[1:], cast_stacks)]


def _t5_bucket(rel):
    half = N_BUCKETS // 2
    max_exact = half // 2
    bucket = jnp.where(rel > 0, half, 0)
    n = jnp.abs(rel)
    nf = jnp.maximum(n, 1).astype(F32)
    large = max_exact + (jnp.log(nf / max_exact) / math.log(MAX_DISTANCE / max_exact)
                         * (half - max_exact)).astype(jnp.int32)
    large = jnp.minimum(large, half - 1)
    return bucket + jnp.where(n < max_exact, n, large)


def _band_bias(rel_bias):
    qi = jnp.arange(BLOCK)[:, None]
    kj = jnp.arange(3 * BLOCK)[None, :]
    rel = kj - BLOCK - qi
    onehot = jax.nn.one_hot(_t5_bucket(rel), N_BUCKETS, dtype=F32)
    bias = jnp.einsum('qkb,bh->qkh', onehot, rel_bias.astype(F32),
                      precision=lax.Precision.HIGHEST)
    bias = jnp.where((jnp.abs(rel) <= WINDOW)[:, :, None], bias, NEG_INF)
    bias = bias.reshape(BLOCK, 3, BLOCK, N_KV_HEADS, GQA).transpose(3, 1, 2, 4, 0)
    bias = bias.reshape(N_KV_HEADS, 3, BLOCK, GQA * BLOCK)
    return jnp.concatenate([bias, jnp.full_like(bias[:, :1], NEG_INF)], axis=1) * LOG2_E


def _attn_kernel(q_ref, kp_ref, km_ref, kn_ref, vp_ref, vm_ref, vn_ref, bias_ref, sink_ref,
                 o_ref, k_scr, v_scr, *, tq):
    i = pl.program_id(1)
    last = pl.num_programs(1) - 1
    k_scr[0:BLOCK] = kp_ref[0]
    k_scr[BLOCK:BLOCK + tq] = km_ref[0]
    k_scr[BLOCK + tq:] = kn_ref[0]
    v_scr[0:BLOCK] = vp_ref[0]
    v_scr[BLOCK:BLOCK + tq] = vm_ref[0]
    v_scr[BLOCK + tq:] = vn_ref[0]
    scale = HEAD_DIM_ATT ** -0.5 * LOG2_E
    nsb = tq // BLOCK
    nt = (((1,), (1,)), ((), ()))
    tn = (((0,), (0,)), ((), ()))
    ones = jnp.ones((3 * BLOCK, HEAD_DIM_ATT), BF16)
    r_id = lax.broadcasted_iota(jnp.int32, (SINK_ROWS, 2 * HEAD_DIM_ATT), 0)
    c_id = lax.broadcasted_iota(jnp.int32, (SINK_ROWS, 2 * HEAD_DIM_ATT), 1)
    sink_rows = jnp.where((r_id == 0) & (c_id >= HEAD_DIM_ATT), 1.0, 0.0).astype(BF16)
    for sb in range(nsb):
        for h in range(N_KV_HEADS):
            c0 = h * HEAD_DIM_ATT
            qs = jnp.concatenate(
                [q_ref[0, sb * BLOCK:(sb + 1) * BLOCK,
                       (h * GQA + g) * HEAD_DIM_ATT:(h * GQA + g + 1) * HEAD_DIM_ATT]
                 for g in range(GQA)], axis=0)
            parts = []
            for c in range(3):
                kc = k_scr[(sb + c) * BLOCK:(sb + c + 1) * BLOCK, c0:c0 + HEAD_DIM_ATT]
                if c == 0 and sb == 0:
                    slot = jnp.where(i == 0, 3, 0)
                elif c == 2 and sb == nsb - 1:
                    slot = jnp.where(i == last, 3, 2)
                else:
                    slot = c
                s_c = lax.dot_general(kc, qs, nt, preferred_element_type=F32)
                parts.append(s_c * scale + bias_ref[h, slot])
            s = jnp.concatenate(parts, axis=0)
            sink = sink_ref[h:h + 1, :]
            m = jnp.maximum(jnp.max(s, axis=0, keepdims=True), sink)
            p = jnp.exp2(s - m).astype(BF16)
            p_sink = jnp.broadcast_to(jnp.exp2(sink - m).astype(BF16), (SINK_ROWS, GQA * BLOCK))
            p_ext = jnp.concatenate([p, p_sink], axis=0)
            vb = v_scr[sb * BLOCK:(sb + 3) * BLOCK, c0:c0 + HEAD_DIM_ATT]
            v_ext = jnp.concatenate([jnp.concatenate([vb, ones], axis=1), sink_rows], axis=0)
            o_ext = lax.dot_general(p_ext, v_ext, tn, preferred_element_type=F32)
            o = o_ext[:, :HEAD_DIM_ATT] / o_ext[:, HEAD_DIM_ATT:]
            for g in range(GQA):
                o_ref[0, sb * BLOCK:(sb + 1) * BLOCK,
                      (h * GQA + g) * HEAD_DIM_ATT:(h * GQA + g + 1) * HEAD_DIM_ATT] = (
                    o[g * BLOCK:(g + 1) * BLOCK].astype(o_ref.dtype))


def window_attention(proj, band_bias, sink):
    b, s, _ = proj.shape
    tq = min(ATT_Q_TILE, s)
    nblk = s // BLOCK
    r = tq // BLOCK
    assert r >= 2 and s % tq == 0
    k_col = ATT_WIDTH // KV_WIDTH
    v_col = k_col + 1

    def main(col):
        return pl.BlockSpec((1, tq, KV_WIDTH), lambda bi, i: (bi, i, col))

    def prev(col):
        return pl.BlockSpec((1, BLOCK, KV_WIDTH), lambda bi, i: (bi, jnp.maximum(i * r - 1, 0), col))

    def nxt(col):
        return pl.BlockSpec((1, BLOCK, KV_WIDTH),
                            lambda bi, i: (bi, jnp.minimum((i + 1) * r, nblk - 1), col))

    sink_rows = jnp.repeat(sink.astype(F32) * LOG2_E, BLOCK).reshape(N_KV_HEADS, GQA * BLOCK)
    return pl.pallas_call(
        functools.partial(_attn_kernel, tq=tq),
        grid=(b, s // tq),
        in_specs=[
            pl.BlockSpec((1, tq, ATT_WIDTH), lambda bi, i: (bi, i, 0)),
            prev(k_col), main(k_col), nxt(k_col),
            prev(v_col), main(v_col), nxt(v_col),
            pl.BlockSpec((N_KV_HEADS, 4, BLOCK, GQA * BLOCK), lambda bi, i: (0, 0, 0, 0)),
            pl.BlockSpec((N_KV_HEADS, GQA * BLOCK), lambda bi, i: (0, 0)),
        ],
        out_specs=pl.BlockSpec((1, tq, ATT_WIDTH), lambda bi, i: (bi, i, 0)),
        out_shape=jax.ShapeDtypeStruct((b, s, ATT_WIDTH), BF16),
        scratch_shapes=[pltpu.VMEM((tq + 2 * BLOCK, KV_WIDTH), BF16),
                        pltpu.VMEM((tq + 2 * BLOCK, KV_WIDTH), BF16)],
        compiler_params=_params("parallel", "arbitrary"),
        name="window_attention",
    )(proj, proj, proj, proj, proj, proj, proj, band_bias, sink_rows)


def _rotate(x, cos, sin):
    half = RET_HEAD_DIM // 2
    x1 = x[:, :half]
    x2 = x[:, half:]
    return jnp.concatenate([x1 * cos - x2 * sin, x1 * sin + x2 * cos], axis=-1)


def _ret_kernel(ld_ref, q_ref, k_ref, v_ref, g_ref, cos_ref, sin_ref, o_ref,
                oacc, qrot, kbwd, kv_scr, rprev, rstate, *, nseg, seg_len):
    h = pl.program_id(1)
    s = pl.program_id(2)
    cs = RET_CHUNK
    nchunk = seg_len // cs
    ld_f = ld_ref[0, h]
    ld_b = ld_ref[1, h]
    ii = lax.broadcasted_iota(jnp.int32, (cs, 1), 0).astype(F32)
    jj = lax.broadcasted_iota(jnp.int32, (1, cs), 1).astype(F32)
    k_scale = RET_HEAD_DIM ** -0.5
    nt = (((1,), (1,)), ((), ()))
    tn = (((0,), (0,)), ((), ()))
    chunk_len = jnp.full((1, 1), cs, F32)

    @pl.when(s == 0)
    def _():
        rstate[...] = jnp.zeros_like(rstate)

    @pl.when(s < nseg)
    def _():
        base = s * seg_len
        diff = ii - jj
        dmat = k_scale * jnp.where(diff >= 0, jnp.exp(ld_f * jnp.maximum(diff, 0.0)),
                                   jnp.exp(ld_b * jnp.maximum(-diff, 0.0)))
        q_dec = jnp.exp(ld_f * (ii + 1.0))
        k_dec_f = k_scale * jnp.exp(ld_f * (cs - 1.0 - ii))
        k_dec_b = k_scale * jnp.exp(ld_b * ii)
        g_chunk = jnp.exp(chunk_len * ld_f)

        def products(c, carry):
            r0 = pl.multiple_of(c * cs, cs)
            rows = pl.ds(pl.multiple_of(base + r0, cs), cs)
            cos = cos_ref[pl.ds(r0, cs), :]
            sin = sin_ref[pl.ds(r0, cs), :]
            q = _rotate(q_ref[0, pl.ds(r0, cs), :].astype(F32), cos, sin).astype(BF16)
            k = _rotate(k_ref[0, pl.ds(r0, cs), :].astype(F32), cos, sin)
            v = v_ref[0, pl.ds(r0, cs), :]
            qrot[rows, :] = q
            kbwd[rows, :] = (k * k_dec_b).astype(BF16)
            a = lax.dot_general(q, k.astype(BF16), nt, preferred_element_type=F32) * dmat
            oacc[rows, :] = jnp.dot(a.astype(BF16), v, preferred_element_type=F32)
            kv_scr[c] = lax.dot_general((k * k_dec_f).astype(BF16), v, tn, preferred_element_type=F32)
            return carry

        for c_static in range(nchunk):
            products(c_static, 0)

        def scan(c, carry):
            r = rstate[0]
            rprev[c] = r.astype(BF16)
            rstate[0] = g_chunk * r + kv_scr[c]
            return carry

        for c_static in range(nchunk):
            scan(c_static, 0)

        def cross(c, carry):
            rows = pl.ds(pl.multiple_of(base + c * cs, cs), cs)
            oacc[rows, :] += q_dec * jnp.dot(qrot[rows, :], rprev[c], preferred_element_type=F32)
            return carry

        for c_static in range(nchunk):
            cross(c_static, 0)

    @pl.when(s >= nseg)
    def _():
        base = (2 * nseg - 1 - s) * seg_len
        q_dec = jnp.exp(ld_b * (cs - ii))
        g_chunk = jnp.exp(chunk_len * ld_b)

        def products(c, carry):
            r0 = pl.multiple_of(c * cs, cs)
            rows = pl.ds(pl.multiple_of(base + r0, cs), cs)
            kv_scr[c] = lax.dot_general(kbwd[rows, :], v_ref[0, pl.ds(r0, cs), :], tn,
                                        preferred_element_type=F32)
            return carry

        for c_static in range(nchunk):
            products(c_static, 0)

        def scan(t, carry):
            c = nchunk - 1 - t
            r = rstate[1]
            rprev[c] = r.astype(BF16)
            rstate[1] = g_chunk * r + kv_scr[c]
            return carry

        for c_static in range(nchunk):
            scan(c_static, 0)

        def finish(c, carry):
            r0 = pl.multiple_of(c * cs, cs)
            rows = pl.ds(pl.multiple_of(base + r0, cs), cs)
            o = oacc[rows, :] + q_dec * jnp.dot(qrot[rows, :], rprev[c], preferred_element_type=F32)
            mu = jnp.mean(o, axis=-1, keepdims=True)
            oc = o - mu
            var = jnp.mean(oc * oc, axis=-1, keepdims=True)
            gate = g_ref[0, pl.ds(r0, cs), :].astype(F32)
            silu = gate * (1.0 / (1.0 + jnp.exp(-gate)))
            o_ref[0, pl.ds(r0, cs), :] = (oc * lax.rsqrt(var + EPS) * silu).astype(o_ref.dtype)
            return carry

        for c_static in range(nchunk):
            finish(c_static, 0)


def _rotary_tables(seq_len):
    half = RET_HEAD_DIM // 2
    inv = 1.0 / (10000.0 ** jnp.linspace(0.0, 1.0, half, dtype=F32))
    ang = jnp.arange(seq_len, dtype=F32)[:, None] * inv[None, :]
    return jnp.cos(ang), jnp.sin(ang)


def retention(proj, log_decay, cos, sin):
    b, s, _ = proj.shape
    seg_len = min(RET_SEG, s)
    nseg = s // seg_len
    nchunk = seg_len // RET_CHUNK
    base = (ATT_WIDTH + 2 * KV_WIDTH) // RET_HEAD_DIM

    def seg_both(t):
        return jnp.where(t < nseg, t, 2 * nseg - 1 - t)

    def seg_fwd_only(t):
        return jnp.minimum(t, nseg - 1)

    def seg_bwd_only(t):
        return jnp.where(t < nseg, nseg - 1, 2 * nseg - 1 - t)

    def head_block(which, seg):
        off = base + which * N_RET_HEADS
        return pl.BlockSpec((1, seg_len, RET_HEAD_DIM), lambda bi, h, t: (bi, seg(t), off + h))

    table = pl.BlockSpec((seg_len, RET_HEAD_DIM // 2), lambda bi, h, t: (seg_fwd_only(t), 0))
    return pl.pallas_call(
        functools.partial(_ret_kernel, nseg=nseg, seg_len=seg_len),
        grid=(b, N_RET_HEADS, 2 * nseg),
        in_specs=[pl.BlockSpec(memory_space=pltpu.SMEM),
                  head_block(0, seg_fwd_only), head_block(1, seg_fwd_only),
                  head_block(2, seg_both), head_block(3, seg_bwd_only), table, table],
        out_specs=pl.BlockSpec((1, seg_len, RET_HEAD_DIM),
                               lambda bi, h, t: (bi, seg_bwd_only(t), h)),
        out_shape=jax.ShapeDtypeStruct((b, s, RET_WIDTH), BF16),
        scratch_shapes=[pltpu.VMEM((s, RET_HEAD_DIM), F32),
                        pltpu.VMEM((s, RET_HEAD_DIM), BF16),
                        pltpu.VMEM((s, RET_HEAD_DIM), BF16),
                        pltpu.VMEM((nchunk, RET_HEAD_DIM, RET_HEAD_DIM), F32),
                        pltpu.VMEM((nchunk, RET_HEAD_DIM, RET_HEAD_DIM), BF16),
                        pltpu.VMEM((2, RET_HEAD_DIM, RET_HEAD_DIM), F32)],
        compiler_params=_params("parallel", "arbitrary", "arbitrary"),
        name="retention",
    )(log_decay, proj, proj, proj, proj, cos, sin)


def _outproj_kernel(att_ref, ret_ref, x_ref, wa_ref, wr_ref, g_ref, wrt_ref, xo_ref, hn_ref, aff_ref, ybuf):
    i = pl.program_id(0)
    cur = lax.rem(i, 2)

    @pl.when(i == 0)
    def _():
        ybuf[1] = jnp.zeros(ybuf.shape[1:], F32)

    y_prev = ybuf[1 - cur]
    var = jnp.mean(y_prev * y_prev, axis=-1, keepdims=True)
    hn = (y_prev * lax.rsqrt(var + EPS) * g_ref[...]).astype(BF16)
    hn_ref[...] = hn
    logits = jnp.dot(hn, wrt_ref[...], preferred_element_type=F32)
    lane = lax.broadcasted_iota(jnp.int32, (1, LANES), 1)
    logits = jnp.where(lane < N_EXPERTS, logits, NEG_INF)
    e = jnp.exp(logits - jnp.max(logits, axis=-1, keepdims=True))
    aff = e / jnp.sum(e, axis=-1, keepdims=True)
    aff_ref[...] = jnp.transpose(aff)[:N_EXPERTS]

    y = x_ref[...] + jnp.dot(att_ref[...], wa_ref[...], preferred_element_type=F32)
    y = y + jnp.dot(ret_ref[...], wr_ref[...], preferred_element_type=F32)
    xo_ref[...] = y
    ybuf[cur] = y


def outproj_norm_router(att, ret, x2d, w_out_bf16, g, w_router_pad):
    n, d = x2d.shape
    tm = min(ROW_TILE_OUTPROJ, n)
    nt = n // tm
    row = lambda i: (jnp.minimum(i, nt - 1), 0)
    prev_row = lambda i: (jnp.maximum(i - 1, 0), 0)
    return pl.pallas_call(
        _outproj_kernel,
        grid=(nt + 1,),
        in_specs=[
            pl.BlockSpec((tm, ATT_WIDTH), row),
            pl.BlockSpec((tm, RET_WIDTH), row),
            pl.BlockSpec((tm, d), row),
            pl.BlockSpec((ATT_WIDTH, d), lambda i: (0, 0)),
            pl.BlockSpec((RET_WIDTH, d), lambda i: (1, 0)),
            pl.BlockSpec((1, d), lambda i: (0, 0)),
            pl.BlockSpec((d, LANES), lambda i: (0, 0)),
        ],
        out_specs=[pl.BlockSpec((tm, d), row), pl.BlockSpec((tm, d), prev_row),
                   pl.BlockSpec((N_EXPERTS, tm), lambda i: (0, jnp.maximum(i - 1, 0)))],
        out_shape=[jax.ShapeDtypeStruct((n, d), F32), jax.ShapeDtypeStruct((n, d), BF16),
                   jax.ShapeDtypeStruct((N_EXPERTS, n), F32)],
        scratch_shapes=[pltpu.VMEM((2, tm, d), F32)],
        compiler_params=_params("arbitrary"),
        name="outproj_norm_router",
    )(att, ret, x2d, w_out_bf16, w_out_bf16, g.reshape(1, d), w_router_pad)


def _moe_kernel(xe_ref, wg_ref, wu_ref, wd_ref, gate_ref, o_ref):
    x = xe_ref[0]
    hg = jnp.dot(x, wg_ref[0], preferred_element_type=F32)
    hu = jnp.dot(x, wu_ref[0], preferred_element_type=F32)
    h = hg * (1.0 / (1.0 + jnp.exp(-hg))) * hu
    y = jnp.dot(h.astype(BF16), wd_ref[0], preferred_element_type=F32)
    gate = jnp.transpose(gate_ref[0, 0])[:, 0:1]
    o_ref[0] = (y * gate).astype(o_ref.dtype)


def expert_ffn(xe, gates, wg, wu, wd):
    e, cap, d = xe.shape
    ff = wg.shape[-1]
    tc = min(MOE_SLOT_TILE, cap)
    return pl.pallas_call(
        _moe_kernel,
        grid=(e, cap // tc),
        in_specs=[
            pl.BlockSpec((1, tc, d), lambda ei, t: (ei, t, 0)),
            pl.BlockSpec((1, d, ff), lambda ei, t: (ei, 0, 0)),
            pl.BlockSpec((1, d, ff), lambda ei, t: (ei, 0, 0)),
            pl.BlockSpec((1, ff, d), lambda ei, t: (ei, 0, 0)),
            pl.BlockSpec((1, 1, GATE_ROWS, tc), lambda ei, t: (ei, t, 0, 0)),
        ],
        out_specs=pl.BlockSpec((1, tc, d), lambda ei, t: (ei, t, 0)),
        out_shape=jax.ShapeDtypeStruct((e, cap, d), BF16),
        compiler_params=_params("parallel", "arbitrary"),
        name="expert_ffn",
    )(xe, wg, wu, wd, jnp.broadcast_to(gates.reshape(e, cap // tc, 1, tc), (e, cap // tc, GATE_ROWS, tc)))


def _combine_kernel(lo_ref, hi_ref, x_ref, tok_ref, g_ref, ye_hbm, o_ref, buf, sem, *, tile, blk, apply_norm):
    i = pl.program_id(0)
    lo = lo_ref[i]
    hi = hi_ref[i]
    bank = lax.rem(i, 2)
    t0 = i * tile

    def fetch(block, bk, k):
        slot = lax.rem(k, COMBINE_BUFFERS)
        return pltpu.make_async_copy(ye_hbm.at[pl.ds(pl.multiple_of(block * blk, blk), blk)],
                                     buf.at[bk, slot], sem.at[bk, slot])

    def start_first_blocks(first, end, bk):
        for k in range(COMBINE_BUFFERS):
            @pl.when(first + k < end)
            def _():
                fetch(first + k, bk, k).start()

    @pl.when(i == 0)
    def _():
        start_first_blocks(lo, hi, bank)

    @pl.when(i + 1 < pl.num_programs(0))
    def _():
        start_first_blocks(lo_ref[i + 1], hi_ref[i + 1], 1 - bank)

    o_ref[...] = x_ref[...]
    row = lax.broadcasted_iota(jnp.int32, (tile, blk), 0)

    def body(c, carry):
        k = c - lo
        fetch(c, bank, k).wait()
        rel = tok_ref[pl.ds(c, 1), :] - t0
        sel = jnp.where(row == rel, 1.0, 0.0).astype(BF16)
        o_ref[...] += jnp.dot(sel, buf[bank, lax.rem(k, COMBINE_BUFFERS)], preferred_element_type=F32)

        @pl.when(c + COMBINE_BUFFERS < hi)
        def _():
            fetch(c + COMBINE_BUFFERS, bank, k + COMBINE_BUFFERS).start()

        return carry

    lax.fori_loop(lo, hi, body, 0)

    if apply_norm:
        y = o_ref[...]
        var = jnp.mean(y * y, axis=-1, keepdims=True)
        o_ref[...] = y * lax.rsqrt(var + EPS) * g_ref[...]


def combine(x2d, ye_sorted, tok_sorted, final_g):
    n, d = x2d.shape
    rows = ye_sorted.shape[0]
    tile = min(COMBINE_TOKEN_TILE, n)
    blk = min(COMBINE_ROW_BLOCK, rows)
    bounds = jnp.searchsorted(tok_sorted, jnp.arange(0, n + 1, tile, dtype=jnp.int32),
                              method="compare_all").astype(jnp.int32)
    start, end = bounds[:-1], bounds[1:]
    lo = start // blk
    hi = jnp.where(end > start, (end + blk - 1) // blk, lo)
    apply_norm = final_g is not None
    g = (final_g if apply_norm else jnp.ones((d,), F32)).reshape(1, d)
    return pl.pallas_call(
        functools.partial(_combine_kernel, tile=tile, blk=blk, apply_norm=apply_norm),
        grid=(n // tile,),
        in_specs=[pl.BlockSpec(memory_space=pltpu.SMEM), pl.BlockSpec(memory_space=pltpu.SMEM),
                  pl.BlockSpec((tile, d), lambda i: (i, 0)),
                  pl.BlockSpec((rows // blk, blk), lambda i: (0, 0)),
                  pl.BlockSpec((1, d), lambda i: (0, 0)),
                  pl.BlockSpec(memory_space=pl.ANY)],
        out_specs=pl.BlockSpec((tile, d), lambda i: (i, 0)),
        out_shape=jax.ShapeDtypeStruct((n, d), F32),
        scratch_shapes=[pltpu.VMEM((2, COMBINE_BUFFERS, blk, d), BF16),
                        pltpu.SemaphoreType.DMA((2, COMBINE_BUFFERS))],
        compiler_params=_params("arbitrary"),
        name="combine",
    )(lo, hi, x2d, tok_sorted.reshape(rows // blk, blk), g, ye_sorted)


def _top_k_tokens(aff, cap):
    keys = -lax.bitcast_convert_type(aff, jnp.int32)
    tokens = lax.broadcasted_iota(jnp.int32, aff.shape, 1)
    keys, tokens = lax.sort((keys, tokens), dimension=1, is_stable=True, num_keys=1)
    return lax.bitcast_convert_type(-keys[:, :cap], F32), tokens[:, :cap]


def _layer(x2d, bsz, seq, lw, moe_stacks, moe_cache, band_bias, cos, sin, final_g):
    n = bsz * seq
    layer = lw["layer"]
    if layer in moe_cache:
        proj, _ = norm_inproj(x2d, lw["norm_mix_g"], lw["w_in"])
    elif all(side_cast_fits(n, st) for st in moe_stacks):
        proj, moe_cache[layer] = norm_inproj(x2d, lw["norm_mix_g"], lw["w_in"], moe_stacks, layer)
    else:
        proj, _ = norm_inproj(x2d, lw["norm_mix_g"], lw["w_in"])
        moe_cache[layer] = [layer_weight_bf16(st, layer) for st in moe_stacks]
    w_gate, w_up, w_down = moe_cache[layer]
    proj = proj.reshape(bsz, seq, IN_WIDTH)
    att = window_attention(proj, band_bias, lw["attn_sink"]).reshape(n, ATT_WIDTH)
    ret = retention(proj, lw["log_decay"], cos, sin).reshape(n, RET_WIDTH)
    x2d, hn, aff = outproj_norm_router(att, ret, x2d, lw["w_out"], lw["norm_ffn_g"], lw["w_router"])
    cap = CAPACITY_FACTOR * n // N_EXPERTS
    gates, idx = _top_k_tokens(aff, cap)
    ye = expert_ffn(hn[idx], gates, w_gate, w_up, w_down)
    slots = N_EXPERTS * cap
    tok_sorted, perm = lax.sort_key_val(idx.reshape(slots).astype(jnp.int32),
                                        jnp.arange(slots, dtype=jnp.int32))
    return combine(x2d, ye.reshape(slots, D_MODEL)[perm], tok_sorted, final_g)


def kernel(x_prompt, x_sample, norm_mix_g, w_in, attn_sink, rel_bias, ret_decay_fwd, ret_decay_bwd,
           w_out, norm_ffn_g, w_router, w_gate, w_up, w_down, norm_final_g):
    depth = w_in.shape[0]
    band_bias = _band_bias(rel_bias)
    layers = []
    for l in range(depth):
        layers.append({
            "layer": l,
            "norm_mix_g": norm_mix_g[l],
            "w_in": layer_weight_bf16(w_in, l),
            "attn_sink": attn_sink[l],
            "log_decay": jnp.stack([jax.nn.log_sigmoid(ret_decay_fwd[l].astype(F32)),
                                    jax.nn.log_sigmoid(ret_decay_bwd[l].astype(F32))]),
            "w_out": layer_weight_bf16(w_out, l),
            "norm_ffn_g": norm_ffn_g[l],
            "w_router": jnp.pad(w_router[l].astype(BF16), ((0, 0), (0, LANES - N_EXPERTS))),
        })
    moe_stacks = (w_gate, w_up, w_down)
    moe_cache = {}

    def trunk(x):
        bsz, seq, d = x.shape
        cos, sin = _rotary_tables(seq)
        x2d = x.reshape(bsz * seq, d)
        for l, lw in enumerate(layers):
            x2d = _layer(x2d, bsz, seq, lw, moe_stacks, moe_cache, band_bias, cos, sin,
                         norm_final_g if l == depth - 1 else None)
        return x2d.reshape(bsz, seq, d)

    return (trunk(x_prompt), trunk(x_sample))
```

```python
import functools
import math

import jax
import jax.numpy as jnp
from jax import lax
from jax.experimental import pallas as pl
from jax.experimental.pallas import tpu as pltpu

D_MODEL = 2048
HEAD_DIM_ATT = 128
N_Q_HEADS = 8
N_KV_HEADS = 2
GQA = N_Q_HEADS // N_KV_HEADS
ATT_WIDTH = N_Q_HEADS * HEAD_DIM_ATT
KV_WIDTH = N_KV_HEADS * HEAD_DIM_ATT
WINDOW = 128
BLOCK = 128
N_BUCKETS = 32
MAX_DISTANCE = 128
N_RET_HEADS = 4
RET_HEAD_DIM = 256
RET_WIDTH = N_RET_HEADS * RET_HEAD_DIM
RET_CHUNK = 256
IN_WIDTH = ATT_WIDTH + 2 * KV_WIDTH + 4 * RET_WIDTH
N_EXPERTS = 16
CAPACITY_FACTOR = 2
EXPERT_FF = D_MODEL // 2
EPS = 1e-6
NEG_INF = -1e30
LANES = 128
LOG2_E = math.log2(math.e)
SINK_ROWS = 16

VMEM_LIMIT_BYTES = 56 * 1024 * 1024

INPROJ_ROW_TILE = 512
INPROJ_ROW_TILE_WITH_CAST = 256
ROW_TILE_OUTPROJ = 512
ATT_Q_TILE = 512
RET_SEG = 2048
MOE_SLOT_TILE = 512
GATE_ROWS = 8
COMBINE_TOKEN_TILE = 512
CAST_BLOCK_ELEMS = 1024 * 1024
SIDE_CAST_BLOCK_ELEMS = 512 * 1024
COMBINE_ROW_BLOCK = 256
COMBINE_BUFFERS = 4

BF16 = jnp.bfloat16
F32 = jnp.float32


def _params(*sem):
    return pltpu.CompilerParams(dimension_semantics=sem, vmem_limit_bytes=VMEM_LIMIT_BYTES)


def _cast_kernel(w_ref, o_ref):
    o_ref[...] = w_ref[0].astype(o_ref.dtype)


def layer_weight_bf16(w_stacked, layer):
    shape = w_stacked.shape[1:]
    cols = shape[-1]
    rows = math.prod(shape[:-1])
    tr = rows
    while tr * cols > CAST_BLOCK_ELEMS and tr % 32 == 0:
        tr //= 2
    out = pl.pallas_call(
        _cast_kernel,
        grid=(rows // tr,),
        in_specs=[pl.BlockSpec((1, tr, cols), lambda i: (layer, i, 0))],
        out_specs=pl.BlockSpec((tr, cols), lambda i: (i, 0)),
        out_shape=jax.ShapeDtypeStruct((rows, cols), BF16),
        compiler_params=_params("parallel"),
        name="weight_cast",
    )(w_stacked.reshape(w_stacked.shape[0], rows, cols))
    return out.reshape(shape)


def _norm_inproj_kernel(x_ref, g_ref, w_ref, *rest, n_cast):
    cast_in = rest[:n_cast]
    o_ref = rest[n_cast]
    cast_out = rest[n_cast + 1:2 * n_cast + 1]
    hbuf = rest[2 * n_cast + 1]
    i = pl.program_id(0)

    @pl.when(i == 0)
    def _():
        hbuf[1] = jnp.zeros(hbuf.shape[1:], BF16)

    def step(cur):
        o_ref[...] = jnp.dot(hbuf[1 - cur], w_ref[...], preferred_element_type=F32).astype(o_ref.dtype)
        x = x_ref[...]
        var = jnp.mean(x * x, axis=-1, keepdims=True)
        hbuf[cur] = (x * lax.rsqrt(var + EPS) * g_ref[...]).astype(BF16)
        for src, dst in zip(cast_in, cast_out):
            dst[...] = src[0].astype(dst.dtype)

    @pl.when(lax.rem(i, 2) == 0)
    def _():
        step(0)

    @pl.when(lax.rem(i, 2) == 1)
    def _():
        step(1)


def _inproj_row_tile(n, casting):
    return min(INPROJ_ROW_TILE_WITH_CAST if casting else INPROJ_ROW_TILE, n)


def side_cast_fits(n, stacked):
    steps = n // _inproj_row_tile(n, True)
    rows = math.prod(stacked.shape[1:-1])
    per_step = rows // steps
    return (rows % steps == 0 and per_step % 16 == 0
            and per_step * stacked.shape[-1] <= SIDE_CAST_BLOCK_ELEMS)


def norm_inproj(x2d, g, w_bf16, cast_stacks=(), cast_layer=0):
    n, d = x2d.shape
    width = w_bf16.shape[1]
    tm = _inproj_row_tile(n, bool(cast_stacks))
    nt = n // tm
    cur_row = lambda i: (jnp.minimum(i, nt - 1), 0)
    prev_row = lambda i: (jnp.maximum(i - 1, 0), 0)
    in_specs = [
        pl.BlockSpec((tm, d), cur_row),
        pl.BlockSpec((1, d), lambda i: (0, 0)),
        pl.BlockSpec((d, width), lambda i: (0, 0), pipeline_mode=pl.Buffered(1)),
    ]
    out_specs = [pl.BlockSpec((tm, width), prev_row)]
    out_shape = [jax.ShapeDtypeStruct((n, width), BF16)]
    cast_args = []
    for st in cast_stacks:
        cols = st.shape[-1]
        rows = math.prod(st.shape[1:-1])
        per_step = rows // nt
        in_specs.append(pl.BlockSpec((1, per_step, cols), lambda i: (cast_layer, jnp.minimum(i, nt - 1), 0)))
        out_specs.append(pl.BlockSpec((per_step, cols), lambda i: (jnp.minimum(i, nt - 1), 0)))
        out_shape.append(jax.ShapeDtypeStruct((rows, cols), BF16))
        cast_args.append(st.reshape(st.shape[0], rows, cols))
    outs = pl.pallas_call(
        functools.partial(_norm_inproj_kernel, n_cast=len(cast_stacks)),
        grid=(nt + 1,),
        in_specs=in_specs,
        out_specs=out_specs,
        out_shape=out_shape,
        scratch_shapes=[pltpu.VMEM((2, tm, d), BF16)],
        compiler_params=_params("arbitrary"),
        name="norm_inproj",
    )(x2d, g.reshape(1, d), w_bf16, *cast_args)
    return outs[0], [o.reshape(st.shape[1:]) for o, st in zip(outs[1:], cast_stacks)]


def _t5_bucket(rel):
    half = N_BUCKETS // 2
    max_exact = half // 2
    bucket = jnp.where(rel > 0, half, 0)
    n = jnp.abs(rel)
    nf = jnp.maximum(n, 1).astype(F32)
    large = max_exact + (jnp.log(nf / max_exact) / math.log(MAX_DISTANCE / max_exact)
                         * (half - max_exact)).astype(jnp.int32)
    large = jnp.minimum(large, half - 1)
    return bucket + jnp.where(n < max_exact, n, large)


def _band_bias(rel_bias):
    qi = jnp.arange(BLOCK)[:, None]
    kj = jnp.arange(3 * BLOCK)[None, :]
    rel = kj - BLOCK - qi
    onehot = jax.nn.one_hot(_t5_bucket(rel), N_BUCKETS, dtype=F32)
    bias = jnp.einsum('qkb,bh->qkh', onehot, rel_bias.astype(F32),
                      precision=lax.Precision.HIGHEST)
    bias = jnp.where((jnp.abs(rel) <= WINDOW)[:, :, None], bias, NEG_INF)
    bias = bias.reshape(BLOCK, 3, BLOCK, N_KV_HEADS, GQA).transpose(3, 1, 2, 4, 0)
    bias = bias.reshape(N_KV_HEADS, 3, BLOCK, GQA * BLOCK)
    return jnp.concatenate([bias, jnp.full_like(bias[:, :1], NEG_INF)], axis=1) * LOG2_E


def _attn_kernel(q_ref, kp_ref, km_ref, kn_ref, vp_ref, vm_ref, vn_ref, bias_ref, sink_ref,
                 o_ref, k_scr, v_scr, *, tq):
    i = pl.program_id(1)
    last = pl.num_programs(1) - 1
    k_scr[0:BLOCK] = kp_ref[0]
    k_scr[BLOCK:BLOCK + tq] = km_ref[0]
    k_scr[BLOCK + tq:] = kn_ref[0]
    v_scr[0:BLOCK] = vp_ref[0]
    v_scr[BLOCK:BLOCK + tq] = vm_ref[0]
    v_scr[BLOCK + tq:] = vn_ref[0]
    scale = HEAD_DIM_ATT ** -0.5 * LOG2_E
    nsb = tq // BLOCK
    nt = (((1,), (1,)), ((), ()))
    tn = (((0,), (0,)), ((), ()))
    ones = jnp.ones((3 * BLOCK, HEAD_DIM_ATT), BF16)
    r_id = lax.broadcasted_iota(jnp.int32, (SINK_ROWS, 2 * HEAD_DIM_ATT), 0)
    c_id = lax.broadcasted_iota(jnp.int32, (SINK_ROWS, 2 * HEAD_DIM_ATT), 1)
    sink_rows = jnp.where((r_id == 0) & (c_id >= HEAD_DIM_ATT), 1.0, 0.0).astype(BF16)
    for sb in range(nsb):
        for h in range(N_KV_HEADS):
            c0 = h * HEAD_DIM_ATT
            qs = jnp.concatenate(
                [q_ref[0, sb * BLOCK:(sb + 1) * BLOCK,
                       (h * GQA + g) * HEAD_DIM_ATT:(h * GQA + g + 1) * HEAD_DIM_ATT]
                 for g in range(GQA)], axis=0)
            parts = []
            for c in range(3):
                kc = k_scr[(sb + c) * BLOCK:(sb + c + 1) * BLOCK, c0:c0 + HEAD_DIM_ATT]
                if c == 0 and sb == 0:
                    slot = jnp.where(i == 0, 3, 0)
                elif c == 2 and sb == nsb - 1:
                    slot = jnp.where(i == last, 3, 2)
                else:
                    slot = c
                s_c = lax.dot_general(kc, qs, nt, preferred_element_type=F32)
                parts.append(s_c * scale + bias_ref[h, slot])
            s = jnp.concatenate(parts, axis=0)
            sink = sink_ref[h:h + 1, :]
            m = jnp.maximum(jnp.max(s, axis=0, keepdims=True), sink)
            p = jnp.exp2(s - m).astype(BF16)
            p_sink = jnp.broadcast_to(jnp.exp2(sink - m).astype(BF16), (SINK_ROWS, GQA * BLOCK))
            p_ext = jnp.concatenate([p, p_sink], axis=0)
            vb = v_scr[sb * BLOCK:(sb + 3) * BLOCK, c0:c0 + HEAD_DIM_ATT]
            v_ext = jnp.concatenate([jnp.concatenate([vb, ones], axis=1), sink_rows], axis=0)
            o_ext = lax.dot_general(p_ext, v_ext, tn, preferred_element_type=F32)
            o = o_ext[:, :HEAD_DIM_ATT] / o_ext[:, HEAD_DIM_ATT:]
            for g in range(GQA):
                o_ref[0, sb * BLOCK:(sb + 1) * BLOCK,
                      (h * GQA + g) * HEAD_DIM_ATT:(h * GQA + g + 1) * HEAD_DIM_ATT] = (
                    o[g * BLOCK:(g + 1) * BLOCK].astype(o_ref.dtype))


def window_attention(proj, band_bias, sink):
    b, s, _ = proj.shape
    tq = min(ATT_Q_TILE, s)
    nblk = s // BLOCK
    r = tq // BLOCK
    assert r >= 2 and s % tq == 0
    k_col = ATT_WIDTH // KV_WIDTH
    v_col = k_col + 1

    def main(col):
        return pl.BlockSpec((1, tq, KV_WIDTH), lambda bi, i: (bi, i, col))

    def prev(col):
        return pl.BlockSpec((1, BLOCK, KV_WIDTH), lambda bi, i: (bi, jnp.maximum(i * r - 1, 0), col))

    def nxt(col):
        return pl.BlockSpec((1, BLOCK, KV_WIDTH),
                            lambda bi, i: (bi, jnp.minimum((i + 1) * r, nblk - 1), col))

    sink_rows = jnp.repeat(sink.astype(F32) * LOG2_E, BLOCK).reshape(N_KV_HEADS, GQA * BLOCK)
    return pl.pallas_call(
        functools.partial(_attn_kernel, tq=tq),
        grid=(b, s // tq),
        in_specs=[
            pl.BlockSpec((1, tq, ATT_WIDTH), lambda bi, i: (bi, i, 0)),
            prev(k_col), main(k_col), nxt(k_col),
            prev(v_col), main(v_col), nxt(v_col),
            pl.BlockSpec((N_KV_HEADS, 4, BLOCK, GQA * BLOCK), lambda bi, i: (0, 0, 0, 0)),
            pl.BlockSpec((N_KV_HEADS, GQA * BLOCK), lambda bi, i: (0, 0)),
        ],
        out_specs=pl.BlockSpec((1, tq, ATT_WIDTH), lambda bi, i: (bi, i, 0)),
        out_shape=jax.ShapeDtypeStruct((b, s, ATT_WIDTH), BF16),
        scratch_shapes=[pltpu.VMEM((tq + 2 * BLOCK, KV_WIDTH), BF16),
                        pltpu.VMEM((tq + 2 * BLOCK, KV_WIDTH), BF16)],
        compiler_params=_params("parallel", "arbitrary"),
        name="window_attention",
    )(proj, proj, proj, proj, proj, proj, proj, band_bias, sink_rows)


def _rotate(x, cos, sin):
    half = RET_HEAD_DIM // 2
    x1 = x[:, :half]
    x2 = x[:, half:]
    return jnp.concatenate([x1 * cos - x2 * sin, x1 * sin + x2 * cos], axis=-1)


def _ret_kernel(ld_ref, q_ref, k_ref, v_ref, g_ref, cos_ref, sin_ref, o_ref,
                oacc, qrot, kbwd, kv_scr, rprev, rstate, *, nseg, seg_len):
    h = pl.program_id(1)
    s = pl.program_id(2)
    cs = RET_CHUNK
    nchunk = seg_len // cs
    ld_f = ld_ref[0, h]
    ld_b = ld_ref[1, h]
    ii = lax.broadcasted_iota(jnp.int32, (cs, 1), 0).astype(F32)
    jj = lax.broadcasted_iota(jnp.int32, (1, cs), 1).astype(F32)
    k_scale = RET_HEAD_DIM ** -0.5
    nt = (((1,), (1,)), ((), ()))
    tn = (((0,), (0,)), ((), ()))
    chunk_len = jnp.full((1, 1), cs, F32)

    @pl.when(s == 0)
    def _():
        rstate[...] = jnp.zeros_like(rstate)

    @pl.when(s < nseg)
    def _():
        base = s * seg_len
        diff = ii - jj
        dmat = k_scale * jnp.where(diff >= 0, jnp.exp(ld_f * jnp.maximum(diff, 0.0)),
                                   jnp.exp(ld_b * jnp.maximum(-diff, 0.0)))
        q_dec = jnp.exp(ld_f * (ii + 1.0))
        k_dec_f = k_scale * jnp.exp(ld_f * (cs - 1.0 - ii))
        k_dec_b = k_scale * jnp.exp(ld_b * ii)
        g_chunk = jnp.exp(chunk_len * ld_f)

        def products(c, carry):
            r0 = pl.multiple_of(c * cs, cs)
            rows = pl.ds(pl.multiple_of(base + r0, cs), cs)
            cos = cos_ref[pl.ds(r0, cs), :]
            sin = sin_ref[pl.ds(r0, cs), :]
            q = _rotate(q_ref[0, pl.ds(r0, cs), :].astype(F32), cos, sin).astype(BF16)
            k = _rotate(k_ref[0, pl.ds(r0, cs), :].astype(F32), cos, sin)
            v = v_ref[0, pl.ds(r0, cs), :]
            qrot[rows, :] = q
            kbwd[rows, :] = (k * k_dec_b).astype(BF16)
            a = lax.dot_general(q, k.astype(BF16), nt, preferred_element_type=F32) * dmat
            oacc[rows, :] = jnp.dot(a.astype(BF16), v, preferred_element_type=F32)
            kv_scr[c] = lax.dot_general((k * k_dec_f).astype(BF16), v, tn, preferred_element_type=F32)
            return carry

        for c_static in range(nchunk):
            products(c_static, 0)

        def scan(c, carry):
            r = rstate[0]
            rprev[c] = r.astype(BF16)
            rstate[0] = g_chunk * r + kv_scr[c]
            return carry

        for c_static in range(nchunk):
            scan(c_static, 0)

        def cross(c, carry):
            rows = pl.ds(pl.multiple_of(base + c * cs, cs), cs)
            oacc[rows, :] += q_dec * jnp.dot(qrot[rows, :], rprev[c], preferred_element_type=F32)
            return carry

        for c_static in range(nchunk):
            cross(c_static, 0)

    @pl.when(s >= nseg)
    def _():
        base = (2 * nseg - 1 - s) * seg_len
        q_dec = jnp.exp(ld_b * (cs - ii))
        g_chunk = jnp.exp(chunk_len * ld_b)

        def products(c, carry):
            r0 = pl.multiple_of(c * cs, cs)
            rows = pl.ds(pl.multiple_of(base + r0, cs), cs)
            kv_scr[c] = lax.dot_general(kbwd[rows, :], v_ref[0, pl.ds(r0, cs), :], tn,
                                        preferred_element_type=F32)
            return carry

        for c_static in range(nchunk):
            products(c_static, 0)

        def scan(t, carry):
            c = nchunk - 1 - t
            r = rstate[1]
            rprev[c] = r.astype(BF16)
            rstate[1] = g_chunk * r + kv_scr[c]
            return carry

        for c_static in range(nchunk):
            scan(c_static, 0)

        def finish(c, carry):
            r0 = pl.multiple_of(c * cs, cs)
            rows = pl.ds(pl.multiple_of(base + r0, cs), cs)
            o = oacc[rows, :] + q_dec * jnp.dot(qrot[rows, :], rprev[c], preferred_element_type=F32)
            mu = jnp.mean(o, axis=-1, keepdims=True)
            oc = o - mu
            var = jnp.mean(oc * oc, axis=-1, keepdims=True)
            gate = g_ref[0, pl.ds(r0, cs), :].astype(F32)
            silu = gate * (1.0 / (1.0 + jnp.exp(-gate)))
            o_ref[0, pl.ds(r0, cs), :] = (oc * lax.rsqrt(var + EPS) * silu).astype(o_ref.dtype)
            return carry

        for c_static in range(nchunk):
            finish(c_static, 0)


def _rotary_tables(seq_len):
    half = RET_HEAD_DIM // 2
    inv = 1.0 / (10000.0 ** jnp.linspace(0.0, 1.0, half, dtype=F32))
    ang = jnp.arange(seq_len, dtype=F32)[:, None] * inv[None, :]
    return jnp.cos(ang), jnp.sin(ang)


def retention(proj, log_decay, cos, sin):
    b, s, _ = proj.shape
    seg_len = min(RET_SEG, s)
    nseg = s // seg_len
    nchunk = seg_len // RET_CHUNK
    base = (ATT_WIDTH + 2 * KV_WIDTH) // RET_HEAD_DIM

    def seg_both(t):
        return jnp.where(t < nseg, t, 2 * nseg - 1 - t)

    def seg_fwd_only(t):
        return jnp.minimum(t, nseg - 1)

    def seg_bwd_only(t):
        return jnp.where(t < nseg, nseg - 1, 2 * nseg - 1 - t)

    def head_block(which, seg):
        off = base + which * N_RET_HEADS
        return pl.BlockSpec((1, seg_len, RET_HEAD_DIM), lambda bi, h, t: (bi, seg(t), off + h))

    table = pl.BlockSpec((seg_len, RET_HEAD_DIM // 2), lambda bi, h, t: (seg_fwd_only(t), 0))
    return pl.pallas_call(
        functools.partial(_ret_kernel, nseg=nseg, seg_len=seg_len),
        grid=(b, N_RET_HEADS, 2 * nseg),
        in_specs=[pl.BlockSpec(memory_space=pltpu.SMEM),
                  head_block(0, seg_fwd_only), head_block(1, seg_fwd_only),
                  head_block(2, seg_both), head_block(3, seg_bwd_only), table, table],
        out_specs=pl.BlockSpec((1, seg_len, RET_HEAD_DIM),
                               lambda bi, h, t: (bi, seg_bwd_only(t), h)),
        out_shape=jax.ShapeDtypeStruct((b, s, RET_WIDTH), BF16),
        scratch_shapes=[pltpu.VMEM((s, RET_HEAD_DIM), F32),
                        pltpu.VMEM((s, RET_HEAD_DIM), BF16),
                        pltpu.VMEM((s, RET_HEAD_DIM), BF16),
                        pltpu.VMEM((nchunk, RET_HEAD_DIM, RET_HEAD_DIM), F32),
                        pltpu.VMEM((nchunk, RET_HEAD_DIM, RET_HEAD_DIM), BF16),
                        pltpu.VMEM((2, RET_HEAD_DIM, RET_HEAD_DIM), F32)],
        compiler_params=_params("parallel", "arbitrary", "arbitrary"),
        name="retention",
    )(log_decay, proj, proj, proj, proj, cos, sin)


def _outproj_kernel(att_ref, ret_ref, x_ref, wa_ref, wr_ref, g_ref, wrt_ref, xo_ref, hn_ref, aff_ref, ybuf):
    i = pl.program_id(0)
    cur = lax.rem(i, 2)

    @pl.when(i == 0)
    def _():
        ybuf[1] = jnp.zeros(ybuf.shape[1:], F32)

    y_prev = ybuf[1 - cur]
    var = jnp.mean(y_prev * y_prev, axis=-1, keepdims=True)
    hn = (y_prev * lax.rsqrt(var + EPS) * g_ref[...]).astype(BF16)
    hn_ref[...] = hn
    logits = jnp.dot(hn, wrt_ref[...], preferred_element_type=F32)
    lane = lax.broadcasted_iota(jnp.int32, (1, LANES), 1)
    logits = jnp.where(lane < N_EXPERTS, logits, NEG_INF)
    e = jnp.exp(logits - jnp.max(logits, axis=-1, keepdims=True))
    aff = e / jnp.sum(e, axis=-1, keepdims=True)
    aff_ref[...] = jnp.transpose(aff)[:N_EXPERTS]

    y = x_ref[...] + jnp.dot(att_ref[...], wa_ref[...], preferred_element_type=F32)
    y = y + jnp.dot(ret_ref[...], wr_ref[...], preferred_element_type=F32)
    xo_ref[...] = y
    ybuf[cur] = y


def outproj_norm_router(att, ret, x2d, w_out_bf16, g, w_router_pad):
    n, d = x2d.shape
    tm = min(ROW_TILE_OUTPROJ, n)
    nt = n // tm
    row = lambda i: (jnp.minimum(i, nt - 1), 0)
    prev_row = lambda i: (jnp.maximum(i - 1, 0), 0)
    return pl.pallas_call(
        _outproj_kernel,
        grid=(nt + 1,),
        in_specs=[
            pl.BlockSpec((tm, ATT_WIDTH), row),
            pl.BlockSpec((tm, RET_WIDTH), row),
            pl.BlockSpec((tm, d), row),
            pl.BlockSpec((ATT_WIDTH, d), lambda i: (0, 0)),
            pl.BlockSpec((RET_WIDTH, d), lambda i: (1, 0)),
            pl.BlockSpec((1, d), lambda i: (0, 0)),
            pl.BlockSpec((d, LANES), lambda i: (0, 0)),
        ],
        out_specs=[pl.BlockSpec((tm, d), row), pl.BlockSpec((tm, d), prev_row),
                   pl.BlockSpec((N_EXPERTS, tm), lambda i: (0, jnp.maximum(i - 1, 0)))],
        out_shape=[jax.ShapeDtypeStruct((n, d), F32), jax.ShapeDtypeStruct((n, d), BF16),
                   jax.ShapeDtypeStruct((N_EXPERTS, n), F32)],
        scratch_shapes=[pltpu.VMEM((2, tm, d), F32)],
        compiler_params=_params("arbitrary"),
        name="outproj_norm_router",
    )(att, ret, x2d, w_out_bf16, w_out_bf16, g.reshape(1, d), w_router_pad)


def _moe_kernel(xe_ref, wg_ref, wu_ref, wd_ref, gate_ref, o_ref):
    x = xe_ref[0]
    hg = jnp.dot(x, wg_ref[0], preferred_element_type=F32)
    hu = jnp.dot(x, wu_ref[0], preferred_element_type=F32)
    h = hg * (1.0 / (1.0 + jnp.exp(-hg))) * hu
    y = jnp.dot(h.astype(BF16), wd_ref[0], preferred_element_type=F32)
    gate = jnp.transpose(gate_ref[0, 0])[:, 0:1]
    o_ref[0] = (y * gate).astype(o_ref.dtype)


def expert_ffn(xe, gates, wg, wu, wd):
    e, cap, d = xe.shape
    ff = wg.shape[-1]
    tc = min(MOE_SLOT_TILE, cap)
    return pl.pallas_call(
        _moe_kernel,
        grid=(e, cap // tc),
        in_specs=[
            pl.BlockSpec((1, tc, d), lambda ei, t: (ei, t, 0)),
            pl.BlockSpec((1, d, ff), lambda ei, t: (ei, 0, 0)),
            pl.BlockSpec((1, d, ff), lambda ei, t: (ei, 0, 0)),
            pl.BlockSpec((1, ff, d), lambda ei, t: (ei, 0, 0)),
            pl.BlockSpec((1, 1, GATE_ROWS, tc), lambda ei, t: (ei, t, 0, 0)),
        ],
        out_specs=pl.BlockSpec((1, tc, d), lambda ei, t: (ei, t, 0)),
        out_shape=jax.ShapeDtypeStruct((e, cap, d), BF16),
        compiler_params=_params("parallel", "arbitrary"),
        name="expert_ffn",
    )(xe, wg, wu, wd, jnp.broadcast_to(gates.reshape(e, cap // tc, 1, tc), (e, cap // tc, GATE_ROWS, tc)))


def _combine_kernel(lo_ref, hi_ref, x_ref, tok_ref, g_ref, ye_hbm, o_ref, buf, sem, *, tile, blk, apply_norm):
    i = pl.program_id(0)
    lo = lo_ref[i]
    hi = hi_ref[i]
    bank = lax.rem(i, 2)
    t0 = i * tile

    def fetch(block, bk, k):
        slot = lax.rem(k, COMBINE_BUFFERS)
        return pltpu.make_async_copy(ye_hbm.at[pl.ds(pl.multiple_of(block * blk, blk), blk)],
                                     buf.at[bk, slot], sem.at[bk, slot])

    def start_first_blocks(first, end, bk):
        for k in range(COMBINE_BUFFERS):
            @pl.when(first + k < end)
            def _():
                fetch(first + k, bk, k).start()

    @pl.when(i == 0)
    def _():
        start_first_blocks(lo, hi, bank)

    @pl.when(i + 1 < pl.num_programs(0))
    def _():
        start_first_blocks(lo_ref[i + 1], hi_ref[i + 1], 1 - bank)

    o_ref[...] = x_ref[...]
    row = lax.broadcasted_iota(jnp.int32, (tile, blk), 0)

    def body(c, carry):
        k = c - lo
        fetch(c, bank, k).wait()
        rel = tok_ref[pl.ds(c, 1), :] - t0
        sel = jnp.where(row == rel, 1.0, 0.0).astype(BF16)
        o_ref[...] += jnp.dot(sel, buf[bank, lax.rem(k, COMBINE_BUFFERS)], preferred_element_type=F32)

        @pl.when(c + COMBINE_BUFFERS < hi)
        def _():
            fetch(c + COMBINE_BUFFERS, bank, k + COMBINE_BUFFERS).start()

        return carry

    lax.fori_loop(lo, hi, body, 0)

    if apply_norm:
        y = o_ref[...]
        var = jnp.mean(y * y, axis=-1, keepdims=True)
        o_ref[...] = y * lax.rsqrt(var + EPS) * g_ref[...]


def combine(x2d, ye_sorted, tok_sorted, final_g):
    n, d = x2d.shape
    rows = ye_sorted.shape[0]
    tile = min(COMBINE_TOKEN_TILE, n)
    blk = min(COMBINE_ROW_BLOCK, rows)
    bounds = jnp.searchsorted(tok_sorted, jnp.arange(0, n + 1, tile, dtype=jnp.int32),
                              method="compare_all").astype(jnp.int32)
    start, end = bounds[:-1], bounds[1:]
    lo = start // blk
    hi = jnp.where(end > start, (end + blk - 1) // blk, lo)
    apply_norm = final_g is not None
    g = (final_g if apply_norm else jnp.ones((d,), F32)).reshape(1, d)
    return pl.pallas_call(
        functools.partial(_combine_kernel, tile=tile, blk=blk, apply_norm=apply_norm),
        grid=(n // tile,),
        in_specs=[pl.BlockSpec(memory_space=pltpu.SMEM), pl.BlockSpec(memory_space=pltpu.SMEM),
                  pl.BlockSpec((tile, d), lambda i: (i, 0)),
                  pl.BlockSpec((rows // blk, blk), lambda i: (0, 0)),
                  pl.BlockSpec((1, d), lambda i: (0, 0)),
                  pl.BlockSpec(memory_space=pl.ANY)],
        out_specs=pl.BlockSpec((tile, d), lambda i: (i, 0)),
        out_shape=jax.ShapeDtypeStruct((n, d), F32),
        scratch_shapes=[pltpu.VMEM((2, COMBINE_BUFFERS, blk, d), BF16),
                        pltpu.SemaphoreType.DMA((2, COMBINE_BUFFERS))],
        compiler_params=_params("arbitrary"),
        name="combine",
    )(lo, hi, x2d, tok_sorted.reshape(rows // blk, blk), g, ye_sorted)


def _top_k_tokens(aff, cap):
    keys = -lax.bitcast_convert_type(aff, jnp.int32)
    tokens = lax.broadcasted_iota(jnp.int32, aff.shape, 1)
    keys, tokens = lax.sort((keys, tokens), dimension=1, is_stable=True, num_keys=1)
    return lax.bitcast_convert_type(-keys[:, :cap], F32), tokens[:, :cap]


def _layer(x2d, bsz, seq, lw, moe_stacks, moe_cache, band_bias, cos, sin, final_g):
    n = bsz * seq
    layer = lw["layer"]
    if layer in moe_cache:
        proj, _ = norm_inproj(x2d, lw["norm_mix_g"], lw["w_in"])
    elif all(side_cast_fits(n, st) for st in moe_stacks):
        proj, moe_cache[layer] = norm_inproj(x2d, lw["norm_mix_g"], lw["w_in"], moe_stacks, layer)
    else:
        proj, _ = norm_inproj(x2d, lw["norm_mix_g"], lw["w_in"])
        moe_cache[layer] = [layer_weight_bf16(st, layer) for st in moe_stacks]
    w_gate, w_up, w_down = moe_cache[layer]
    proj = proj.reshape(bsz, seq, IN_WIDTH)
    att = window_attention(proj, band_bias, lw["attn_sink"]).reshape(n, ATT_WIDTH)
    ret = retention(proj, lw["log_decay"], cos, sin).reshape(n, RET_WIDTH)
    x2d, hn, aff = outproj_norm_router(att, ret, x2d, lw["w_out"], lw["norm_ffn_g"], lw["w_router"])
    cap = CAPACITY_FACTOR * n // N_EXPERTS
    gates, idx = _top_k_tokens(aff, cap)
    ye = expert_ffn(hn[idx], gates, w_gate, w_up, w_down)
    slots = N_EXPERTS * cap
    tok_sorted, perm = lax.sort_key_val(idx.reshape(slots).astype(jnp.int32),
                                        jnp.arange(slots, dtype=jnp.int32))
    return combine(x2d, ye.reshape(slots, D_MODEL)[perm], tok_sorted, final_g)


def kernel(x_prompt, x_sample, norm_mix_g, w_in, attn_sink, rel_bias, ret_decay_fwd, ret_decay_bwd,
           w_out, norm_ffn_g, w_router, w_gate, w_up, w_down, norm_final_g):
    depth = w_in.shape[0]
    band_bias = _band_bias(rel_bias)
    layers = []
    for l in range(depth):
        layers.append({
            "layer": l,
            "norm_mix_g": norm_mix_g[l],
            "w_in": layer_weight_bf16(w_in, l),
            "attn_sink": attn_sink[l],
            "log_decay": jnp.stack([jax.nn.log_sigmoid(ret_decay_fwd[l].astype(F32)),
                                    jax.nn.log_sigmoid(ret_decay_bwd[l].astype(F32))]),
            "w_out": layer_weight_bf16(w_out, l),
            "norm_ffn_g": norm_ffn_g[l],
            "w_router": jnp.pad(w_router[l].astype(BF16), ((0, 0), (0, LANES - N_EXPERTS))),
        })
    moe_stacks = (w_gate, w_up, w_down)
    moe_cache = {}

    def trunk(x):
        bsz, seq, d = x.shape
        cos, sin = _rotary_tables(seq)
        x2d = x.reshape(bsz * seq, d)
        for l, lw in enumerate(layers):
            x2d = _layer(x2d, bsz, seq, lw, moe_stacks, moe_cache, band_bias, cos, sin,
                         norm_final_g if l == depth - 1 else None)
        return x2d.reshape(bsz, seq, d)

    return (trunk(x_prompt), trunk(x_sample))
```
